```python
import math
import jax, jax.numpy as jnp
from jax import lax
import numpy as np

D_MODEL = 1024
BATCH = 8
SEQ = 4096
DEPTH = 2

N_MEM = 256
RET_HEADS = 4
RET_DK = D_MODEL // RET_HEADS
RET_DV = 2 * RET_DK
RET_CHUNK = 128
DIFF_HEADS = 4
DIFF_DK = D_MODEL // (2 * DIFF_HEADS)
DIFF_DV = 2 * DIFF_DK
Q_BLOCK = 128
CROSS_HEADS = 4
CROSS_DH = D_MODEL // CROSS_HEADS
D_FF = 4 * D_MODEL
N_BRANCH = 3
EPS = 1e-6

RET_QK_W = RET_HEADS * RET_DK
RET_V_W = RET_HEADS * RET_DV
DIFF_QK_W = DIFF_HEADS * 2 * DIFF_DK
DIFF_V_W = DIFF_HEADS * DIFF_DV
CROSS_W = CROSS_HEADS * CROSS_DH
GATE_W = N_BRANCH * D_MODEL
SPLITS = (RET_QK_W, RET_QK_W, RET_V_W, RET_V_W, DIFF_QK_W, DIFF_QK_W, DIFF_V_W, CROSS_W, GATE_W)
D_IN = RET_QK_W * 2 + RET_V_W * 2 + DIFF_QK_W * 2 + DIFF_V_W + CROSS_W + GATE_W

kernel_name = "hybrid_retention_diffattn_gated_block"


def rms_norm(x, g):
    xf = x.astype(jnp.float32)
    y = xf * lax.rsqrt(jnp.mean(xf * xf, axis=-1, keepdims=True) + EPS)
    return (y * g.astype(jnp.float32)).astype(x.dtype)


def head_norm(x, g, center):
    b, s, h, d = x.shape
    xf = x.astype(jnp.float32)
    if center:
        xf = xf - jnp.mean(xf, axis=-1, keepdims=True)
    y = xf * lax.rsqrt(jnp.mean(xf * xf, axis=-1, keepdims=True) + EPS)
    return (y.reshape(b, s, h * d) * g.astype(jnp.float32)).astype(x.dtype)


def split_cols(z, widths):
    out = []
    off = 0
    for w in widths:
        out.append(z[..., off:off + w])
        off += w
    return out


def retention_chunkwise(q, k, v):
    b, s, h, dk = q.shape
    dv = v.shape[-1]
    c = RET_CHUNK
    n = s // c
    log_g = jnp.log(1.0 - jnp.exp2(-5.0 - jnp.arange(h, dtype=jnp.float32)))
    idx = jnp.arange(c, dtype=jnp.float32)
    rel = idx[:, None] - idx[None, :]
    intra = jnp.where(rel >= 0, jnp.exp(log_g[:, None, None] * jnp.maximum(rel, 0.0)), 0.0)
    in_decay = jnp.exp(log_g[:, None] * (idx + 1.0))
    st_decay = jnp.exp(log_g[:, None] * (c - 1.0 - idx))
    ch_decay = jnp.exp(log_g * c)

    def to_chunks(t):
        return t.reshape(b, n, c, h, t.shape[-1]).transpose(1, 0, 3, 2, 4)

    qc = to_chunks(q)
    kc = to_chunks(k * (dk ** -0.5))
    vc = to_chunks(v)

    def step(state, inp):
        qi, ki, vi = inp
        scores = jnp.einsum("bhid,bhjd->bhij", qi, ki) * intra[None]
        inner = jnp.einsum("bhij,bhje->bhie", scores, vi)
        cross = jnp.einsum("bhid,bhde->bhie", qi, state) * in_decay[None, :, :, None]
        new_state = ch_decay[None, :, None, None] * state + jnp.einsum(
            "bhjd,bhje->bhde", ki * st_decay[None, :, :, None], vi)
        return new_state, inner + cross

    state0 = jnp.zeros((b, h, dk, dv), jnp.float32)
    _, out = lax.scan(step, state0, (qc, kc, vc))
    return out.transpose(1, 0, 3, 2, 4).reshape(b, s, h, dv).astype(q.dtype)


def diff_attention(q, k, v, lam):
    b, s, h, _, d = q.shape
    qt = q.transpose(0, 2, 3, 1, 4)
    kt = k.transpose(0, 2, 3, 1, 4)
    vt = v.transpose(0, 2, 1, 3)
    slopes = jnp.exp2(-8.0 * (jnp.arange(h, dtype=jnp.float32) + 1.0) / h)
    scale = d ** -0.5
    outs = []
    for i in range(s // Q_BLOCK):
        q0 = i * Q_BLOCK
        kend = q0 + Q_BLOCK
        sc = jnp.einsum("bhmqd,bhmkd->bhmqk", qt[:, :, :, q0:kend], kt[:, :, :, :kend]).astype(jnp.float32) * scale
        dist = (jnp.arange(q0, kend)[:, None] - jnp.arange(kend)[None, :]).astype(jnp.float32)
        sc = sc - (slopes[:, None, None] * dist[None])[None, :, None]
        sc = jnp.where(dist[None, None, None] >= 0, sc, -jnp.inf)
        p = jax.nn.softmax(sc, axis=-1)
        pd = p[:, :, 0] - lam * p[:, :, 1]
        outs.append(jnp.einsum("bhqk,bhke->bhqe", pd, vt[:, :, :kend]))
    o = jnp.concatenate(outs, axis=2)
    return o.transpose(0, 2, 1, 3).astype(q.dtype)


def memory_cross_attention(q, mk, mv):
    sc = jnp.einsum("bshd,bmhd->bhsm", q, mk).astype(jnp.float32) * (q.shape[-1] ** -0.5)
    p = jax.nn.softmax(sc, axis=-1)
    return jnp.einsum("bhsm,bmhd->bshd", p, mv).astype(q.dtype)


def setup_inputs(seed: int = 0) -> dict:
    key = jax.random.key(seed)
    ks = jax.random.split(key, 20)
    f32 = jnp.float32
    res_scale = (2.0 * DEPTH) ** -0.5

    def nrm(k, shape, scale):
        return jax.random.normal(k, shape, f32) * scale

    def gain(k, shape):
        return 1.0 + 0.02 * jax.random.normal(k, shape, f32)

    return {
        "x": nrm(ks[0], (BATCH, SEQ, D_MODEL), 1.0),
        "mem": nrm(ks[1], (BATCH, N_MEM, D_MODEL), 1.0),
        "g_mix": gain(ks[2], (DEPTH, D_MODEL)),
        "w_in": nrm(ks[3], (DEPTH, D_MODEL, D_IN), D_MODEL ** -0.5),
        "g_ret": gain(ks[4], (DEPTH, RET_V_W)),
        "w_ret_o": nrm(ks[5], (DEPTH, RET_V_W, D_MODEL), RET_V_W ** -0.5),
        "lambda_q1": nrm(ks[6], (DEPTH, DIFF_DK), 0.1),
        "lambda_k1": nrm(ks[7], (DEPTH, DIFF_DK), 0.1),
        "lambda_q2": nrm(ks[8], (DEPTH, DIFF_DK), 0.1),
        "lambda_k2": nrm(ks[9], (DEPTH, DIFF_DK), 0.1),
        "g_diff": gain(ks[10], (DEPTH, DIFF_V_W)),
        "w_diff_o": nrm(ks[11], (DEPTH, DIFF_V_W, D_MODEL), DIFF_V_W ** -0.5),
        "g_mem": gain(ks[12], (DEPTH, D_MODEL)),
        "w_mem_kv": nrm(ks[13], (DEPTH, D_MODEL, 2 * CROSS_W), D_MODEL ** -0.5),
        "w_cross_o": nrm(ks[14], (DEPTH, CROSS_W, D_MODEL), CROSS_W ** -0.5),
        "w_out": nrm(ks[15], (DEPTH, D_MODEL, D_MODEL), D_MODEL ** -0.5 * res_scale),
        "g_ffn": gain(ks[16], (DEPTH, D_MODEL)),
        "w_up": nrm(ks[17], (DEPTH, D_MODEL, D_FF), D_MODEL ** -0.5),
        "w_down": nrm(ks[18], (DEPTH, D_FF, D_MODEL), D_FF ** -0.5 * res_scale),
        "g_final": gain(ks[19], (D_MODEL,)),
    }


def reference(x, mem, g_mix, w_in, g_ret, w_ret_o, lambda_q1, lambda_k1, lambda_q2, lambda_k2,
              g_diff, w_diff_o, g_mem, w_mem_kv, w_cross_o, w_out, g_ffn, w_up, w_down, g_final):
    b, s, _ = x.shape
    m = mem.shape[1]
    for l in range(DEPTH):
        h = rms_norm(x, g_mix[l])
        z = h @ w_in[l]
        rq, rk, rv, rg, dq, dk, dv, cq, gates = split_cols(z, SPLITS)

        ret = retention_chunkwise(rq.reshape(b, s, RET_HEADS, RET_DK),
                                  rk.reshape(b, s, RET_HEADS, RET_DK),
                                  rv.reshape(b, s, RET_HEADS, RET_DV))
        ret = jax.nn.swish(rg) * head_norm(ret, g_ret[l], True)
        y_ret = ret @ w_ret_o[l]

        lam_init = 0.8 - 0.6 * math.exp(-0.3 * l)
        lam = (jnp.exp(jnp.sum(lambda_q1[l].astype(jnp.float32) * lambda_k1[l].astype(jnp.float32)))
               - jnp.exp(jnp.sum(lambda_q2[l].astype(jnp.float32) * lambda_k2[l].astype(jnp.float32)))
               + lam_init)
        da = diff_attention(dq.reshape(b, s, DIFF_HEADS, 2, DIFF_DK),
                            dk.reshape(b, s, DIFF_HEADS, 2, DIFF_DK),
                            dv.reshape(b, s, DIFF_HEADS, DIFF_DV), lam)
        da = head_norm(da, g_diff[l], False) * (1.0 - lam_init)
        y_diff = da @ w_diff_o[l]

        kv = rms_norm(mem, g_mem[l]) @ w_mem_kv[l]
        mk = kv[..., :CROSS_W].reshape(b, m, CROSS_HEADS, CROSS_DH)
        mv = kv[..., CROSS_W:].reshape(b, m, CROSS_HEADS, CROSS_DH)
        ca = memory_cross_attention(cq.reshape(b, s, CROSS_HEADS, CROSS_DH), mk, mv)
        y_cross = ca.reshape(b, s, CROSS_W) @ w_cross_o[l]

        g = jax.nn.sigmoid(gates.astype(jnp.float32)).astype(x.dtype).reshape(b, s, N_BRANCH, D_MODEL)
        merged = g[:, :, 0] * y_ret + g[:, :, 1] * y_diff + g[:, :, 2] * y_cross
        x = x + merged @ w_out[l]

        u = jax.nn.relu(rms_norm(x, g_ffn[l]) @ w_up[l])
        x = x + (u * u) @ w_down[l]
    return rms_norm(x, g_final)
```

```python
import functools
import math

import jax
import jax.numpy as jnp
from jax import lax
from jax.experimental import pallas as pl
from jax.experimental.pallas import tpu as pltpu

F32 = jnp.float32
BF16 = jnp.bfloat16

D_MODEL = 1024
DEPTH = 2
N_MEM = 256
RET_HEADS = 4
RET_DK = D_MODEL // RET_HEADS
RET_DV = 2 * RET_DK
RET_CHUNK = 128
DIFF_HEADS = 4
DIFF_DK = D_MODEL // (2 * DIFF_HEADS)
DIFF_DV = 2 * DIFF_DK
CROSS_HEADS = 4
CROSS_DH = D_MODEL // CROSS_HEADS
D_FF = 4 * D_MODEL
EPS = 1e-6

RET_QK_W = RET_HEADS * RET_DK
RET_V_W = RET_HEADS * RET_DV
DIFF_QK_W = DIFF_HEADS * 2 * DIFF_DK
DIFF_V_W = DIFF_HEADS * DIFF_DV
CROSS_W = CROSS_HEADS * CROSS_DH
GATE_W = 3 * D_MODEL
OFF_RQ = 0
OFF_RK = OFF_RQ + RET_QK_W
OFF_RV = OFF_RK + RET_QK_W
OFF_RG = OFF_RV + RET_V_W
OFF_DQ = OFF_RG + RET_V_W
OFF_DK = OFF_DQ + DIFF_QK_W
OFF_DV = OFF_DK + DIFF_QK_W
OFF_CQ = OFF_DV + DIFF_V_W
OFF_GATE = OFF_CQ + CROSS_W
D_IN = OFF_GATE + GATE_W

MIB = 1024 * 1024

INPROJ_TM = 1024
INPROJ_TN = 1024
RET_TS = 1024
DIFF_TQ = 512
DIFF_TK = 512
CROSS_TQ = 512
MERGE_TM = 512
FFN_TM = 512
FFN_TF = 1024


def _params(semantics, vmem_mib):
    return pltpu.CompilerParams(dimension_semantics=semantics,
                                vmem_limit_bytes=vmem_mib * MIB)


def _resident(shape, index_map):
    return pl.BlockSpec(shape, index_map, pipeline_mode=pl.Buffered(1))


def _rms(x, g):
    ms = jnp.mean(x * x, axis=-1, keepdims=True)
    return x * lax.rsqrt(ms + EPS) * g


def _sigmoid(x):
    return 1.0 / (1.0 + jnp.exp(-x))


def _dot(a, b):
    return jnp.dot(a, b, preferred_element_type=F32)


def _dot_nt(a, b):
    return lax.dot_general(a, b, (((1,), (1,)), ((), ())), preferred_element_type=F32)


def _dot_tn(a, b):
    return lax.dot_general(a, b, (((0,), (0,)), ((), ())), preferred_element_type=F32)


def _inproj_kernel(x_ref, g_ref, w_ref, cs_ref, o_ref, h_ref):
    @pl.when(pl.program_id(1) == 0)
    def _():
        h_ref[...] = _rms(x_ref[...], g_ref[...]).astype(BF16)

    acc = _dot(h_ref[...], w_ref[...])
    o_ref[...] = (acc * cs_ref[...]).astype(o_ref.dtype)


def _inproj(x, g, w, colscale):
    m, k = x.shape
    n = w.shape[1]
    tm = min(INPROJ_TM, m)
    tn = min(INPROJ_TN, n)
    return pl.pallas_call(
        _inproj_kernel,
        grid=(m // tm, n // tn),
        in_specs=[
            pl.BlockSpec((tm, k), lambda i, j: (i, 0)),
            pl.BlockSpec((1, k), lambda i, j: (0, 0)),
            pl.BlockSpec((k, tn), lambda i, j: (0, j)),
            pl.BlockSpec((1, tn), lambda i, j: (0, j)),
        ],
        out_specs=pl.BlockSpec((tm, tn), lambda i, j: (i, j)),
        out_shape=jax.ShapeDtypeStruct((m, n), BF16),
        scratch_shapes=[pltpu.VMEM((tm, k), BF16)],
        name="inproj",
        compiler_params=_params(("parallel", "arbitrary"), 40),
    )(x, g.reshape(1, k), w, colscale.reshape(1, n))


def _retention_kernel(chd_ref, q_ref, k_ref, v_ref, rg_ref, intra_ref, ind_ref, std_ref,
                      g_ref, o_ref, state_ref):
    h = pl.program_id(1)

    @pl.when(pl.program_id(2) == 0)
    def _():
        state_ref[...] = jnp.zeros_like(state_ref)

    ch_decay = chd_ref[h]
    intra = intra_ref[...]
    in_decay = ind_ref[...]
    st_decay = std_ref[...]
    gain = g_ref[...]
    c = RET_CHUNK

    def chunk(ci, carry):
        rows = pl.ds(pl.multiple_of(ci * c, c), c)
        q = q_ref[rows, :]
        k = k_ref[rows, :]
        v = v_ref[rows, :]
        state = state_ref[...]
        scores = _dot_nt(q, k) * intra
        inner = _dot(scores.astype(BF16), v)
        cross = _dot(q, state.astype(BF16)) * in_decay
        kd = (k.astype(F32) * st_decay).astype(BF16)
        state_ref[...] = ch_decay * state + _dot_tn(kd, v)
        o = inner + cross
        oc = o - jnp.mean(o, axis=-1, keepdims=True)
        y = _rms(oc, gain)
        rg = rg_ref[rows, :].astype(F32)
        o_ref[rows, :] = (rg * _sigmoid(rg) * y).astype(o_ref.dtype)
        return carry

    lax.fori_loop(0, q_ref.shape[0] // c, chunk, 0)


def _retention(z, g_ret, batch, seq):
    hh = RET_HEADS
    c = RET_CHUNK
    ts = RET_TS
    nt = seq // ts
    log_g = jnp.log(1.0 - jnp.exp2(-5.0 - jnp.arange(hh, dtype=F32)))
    idx = jnp.arange(c, dtype=F32)
    rel = idx[:, None] - idx[None, :]
    intra = jnp.where(rel >= 0, jnp.exp(log_g[:, None, None] * jnp.maximum(rel, 0.0)), 0.0)
    in_decay = jnp.exp(log_g[:, None] * (idx + 1.0))[:, :, None]
    st_decay = jnp.exp(log_g[:, None] * (c - 1.0 - idx))[:, :, None]
    ch_decay = jnp.exp(log_g * c)

    def row(b, h, t):
        return b * nt + t

    return pl.pallas_call(
        _retention_kernel,
        grid=(batch, hh, nt),
        in_specs=[
            pl.BlockSpec(memory_space=pltpu.SMEM),
            pl.BlockSpec((ts, RET_DK), lambda b, h, t: (row(b, h, t), OFF_RQ // RET_DK + h)),
            pl.BlockSpec((ts, RET_DK), lambda b, h, t: (row(b, h, t), OFF_RK // RET_DK + h)),
            pl.BlockSpec((ts, RET_DV), lambda b, h, t: (row(b, h, t), OFF_RV // RET_DV + h)),
            pl.BlockSpec((ts, RET_DV), lambda b, h, t: (row(b, h, t), OFF_RG // RET_DV + h)),
            pl.BlockSpec((None, c, c), lambda b, h, t: (h, 0, 0)),
            pl.BlockSpec((None, c, 1), lambda b, h, t: (h, 0, 0)),
            pl.BlockSpec((None, c, 1), lambda b, h, t: (h, 0, 0)),
            pl.BlockSpec((1, RET_DV), lambda b, h, t: (0, h)),
        ],
        out_specs=pl.BlockSpec((ts, RET_DV), lambda b, h, t: (row(b, h, t), h)),
        out_shape=jax.ShapeDtypeStruct((batch * seq, RET_V_W), BF16),
        scratch_shapes=[pltpu.VMEM((RET_DK, RET_DV), F32)],
        name="retention",
        compiler_params=_params(("parallel", "parallel", "arbitrary"), 40),
    )(ch_decay, z, z, z, z, intra, in_decay, st_decay, g_ret.reshape(1, RET_V_W))


def _diff_kernel(lam_init, slopes_ref, q_ref, k_ref, v_ref, lq1_ref, lk1_ref, lq2_ref, lk2_ref,
                 g_ref, o_ref, m_ref, l_ref, acc_ref):
    h = pl.program_id(1)
    qi = pl.program_id(2)
    tq = q_ref.shape[0]
    tk = DIFF_TK
    d = DIFF_DK
    slope = slopes_ref[h]

    m_ref[...] = jnp.full_like(m_ref, -jnp.inf)
    l_ref[...] = jnp.zeros_like(l_ref)
    acc_ref[...] = jnp.zeros_like(acc_ref)

    col = lax.broadcasted_iota(jnp.int32, (1, tk), 1)
    causal = (lax.broadcasted_iota(jnp.int32, (tq, tk), 0)
              >= lax.broadcasted_iota(jnp.int32, (tq, tk), 1))

    def block(j, masked):
        rows = pl.ds(pl.multiple_of(j * tk, tk), tk)
        v = v_ref[rows, :]
        bias = slope * (col + (j - qi) * tk).astype(F32)
        for mp in range(2):
            q = q_ref[:, mp * d:(mp + 1) * d]
            k = k_ref[rows, mp * d:(mp + 1) * d]
            s = _dot_nt(q, k) + bias
            if masked:
                s = jnp.where(causal, s, -jnp.inf)
            m_old = m_ref[mp]
            m_new = jnp.maximum(m_old, jnp.max(s, axis=-1, keepdims=True))
            alpha = jnp.exp(m_old - m_new)
            p = jnp.exp(s - m_new)
            l_ref[mp] = alpha * l_ref[mp] + jnp.sum(p, axis=-1, keepdims=True)
            acc_ref[mp] = alpha * acc_ref[mp] + _dot(p.astype(BF16), v)
            m_ref[mp] = m_new

    def full_block(j, carry):
        block(j, False)
        return carry

    lax.fori_loop(0, qi, full_block, 0)
    block(qi, True)

    lam = (jnp.exp(jnp.sum(lq1_ref[...] * lk1_ref[...], keepdims=True))
           - jnp.exp(jnp.sum(lq2_ref[...] * lk2_ref[...], keepdims=True)) + lam_init)
    o = acc_ref[0] / l_ref[0] - lam * (acc_ref[1] / l_ref[1])
    o_ref[...] = (_rms(o, g_ref[...]) * (1.0 - lam_init)).astype(o_ref.dtype)


def _diff_attention(z, lq1, lk1, lq2, lk2, g_diff, lam_init, batch, seq):
    hh = DIFF_HEADS
    tq = DIFF_TQ
    assert DIFF_TQ == DIFF_TK
    nq = seq // tq
    w = 2 * DIFF_DK
    slopes = jnp.exp2(-8.0 * (jnp.arange(hh, dtype=F32) + 1.0) / hh)
    vec = pl.BlockSpec((1, DIFF_DK), lambda b, h, i: (0, 0))
    return pl.pallas_call(
        functools.partial(_diff_kernel, lam_init),
        grid=(batch, hh, nq),
        in_specs=[
            pl.BlockSpec(memory_space=pltpu.SMEM),
            pl.BlockSpec((tq, w), lambda b, h, i: (b * nq + i, OFF_DQ // w + h)),
            pl.BlockSpec((seq, w), lambda b, h, i: (b, OFF_DK // w + h)),
            pl.BlockSpec((seq, DIFF_DV), lambda b, h, i: (b, OFF_DV // DIFF_DV + h)),
            vec, vec, vec, vec,
            pl.BlockSpec((1, DIFF_DV), lambda b, h, i: (0, h)),
        ],
        out_specs=pl.BlockSpec((tq, DIFF_DV), lambda b, h, i: (b * nq + i, h)),
        out_shape=jax.ShapeDtypeStruct((batch * seq, DIFF_V_W), BF16),
        scratch_shapes=[pltpu.VMEM((2, tq, 1), F32), pltpu.VMEM((2, tq, 1), F32),
                        pltpu.VMEM((2, tq, DIFF_DV), F32)],
        name="diff_attention",
        compiler_params=_params(("parallel", "parallel", "arbitrary"), 40),
    )(slopes, z, z, z, lq1.reshape(1, -1), lk1.reshape(1, -1), lq2.reshape(1, -1),
      lk2.reshape(1, -1), g_diff.reshape(1, DIFF_V_W))


def _cross_kernel(q_ref, mk_ref, mv_ref, o_ref):
    dh = CROSS_DH
    scale = dh ** -0.5
    for h in range(CROSS_HEADS):
        cols = slice(h * dh, (h + 1) * dh)
        s = _dot_nt(q_ref[:, cols], mk_ref[:, cols]) * scale
        p = jnp.exp(s - jnp.max(s, axis=-1, keepdims=True))
        p = p / jnp.sum(p, axis=-1, keepdims=True)
        o_ref[:, cols] = _dot(p.astype(BF16), mv_ref[:, cols]).astype(o_ref.dtype)


def _cross_attention(z, kv, batch, seq):
    tq = CROSS_TQ
    nq = seq // tq
    return pl.pallas_call(
        _cross_kernel,
        grid=(batch, nq),
        in_specs=[
            pl.BlockSpec((tq, CROSS_W), lambda b, i: (b * nq + i, OFF_CQ // CROSS_W)),
            pl.BlockSpec((N_MEM, CROSS_W), lambda b, i: (b, 0)),
            pl.BlockSpec((N_MEM, CROSS_W), lambda b, i: (b, 1)),
        ],
        out_specs=pl.BlockSpec((tq, CROSS_W), lambda b, i: (b * nq + i, 0)),
        out_shape=jax.ShapeDtypeStruct((batch * seq, CROSS_W), BF16),
        name="cross_attention",
        compiler_params=_params(("parallel", "arbitrary"), 40),
    )(z, kv, kv)


def _merge_kernel(x_ref, ret_ref, da_ref, ca_ref, g0_ref, g1_ref, g2_ref,
                  wr_ref, wd_ref, wc_ref, wo_ref, o_ref):
    merged = _sigmoid(g0_ref[...].astype(F32)) * _dot(ret_ref[...], wr_ref[...])
    merged += _sigmoid(g1_ref[...].astype(F32)) * _dot(da_ref[...], wd_ref[...])
    merged += _sigmoid(g2_ref[...].astype(F32)) * _dot(ca_ref[...], wc_ref[...])
    o_ref[...] = x_ref[...] + _dot(merged.astype(BF16), wo_ref[...])


def _merge(x, ret, da, ca, z, w_ret_o, w_diff_o, w_cross_o, w_out):
    m, d = x.shape
    tm = MERGE_TM
    gate0 = OFF_GATE // d

    def rows(width):
        return pl.BlockSpec((tm, width), lambda i: (i, 0))

    def gate(n):
        return pl.BlockSpec((tm, d), lambda i: (i, gate0 + n))

    def weight(k):
        return _resident((k, d), lambda i: (0, 0))

    return pl.pallas_call(
        _merge_kernel,
        grid=(m // tm,),
        in_specs=[rows(d), rows(RET_V_W), rows(DIFF_V_W), rows(CROSS_W),
                  gate(0), gate(1), gate(2),
                  weight(RET_V_W), weight(DIFF_V_W), weight(CROSS_W), weight(d)],
        out_specs=rows(d),
        out_shape=jax.ShapeDtypeStruct((m, d), F32),
        name="merge",
        compiler_params=_params(("parallel",), 48),
    )(x, ret, da, ca, z, z, z, w_ret_o, w_diff_o, w_cross_o, w_out)


def _ffn_kernel(final_norm, x_ref, g_ref, wu_ref, wd_ref, gf_ref, o_ref):
    x = x_ref[...]
    h = _rms(x, g_ref[...]).astype(BF16)
    acc = x
    for f in range(D_FF // FFN_TF):
        cols = slice(f * FFN_TF, (f + 1) * FFN_TF)
        u = jnp.maximum(_dot(h, wu_ref[:, cols]), 0.0)
        acc = acc + _dot((u * u).astype(BF16), wd_ref[cols, :])
    if final_norm:
        acc = _rms(acc, gf_ref[...])
    o_ref[...] = acc


def _ffn(x, g_ffn, w_up, w_down, g_final, final_norm):
    m, d = x.shape
    tm = FFN_TM
    vec = pl.BlockSpec((1, d), lambda i: (0, 0))
    return pl.pallas_call(
        functools.partial(_ffn_kernel, final_norm),
        grid=(m // tm,),
        in_specs=[pl.BlockSpec((tm, d), lambda i: (i, 0)), vec,
                  _resident((d, D_FF), lambda i: (0, 0)),
                  _resident((D_FF, d), lambda i: (0, 0)), vec],
        out_specs=pl.BlockSpec((tm, d), lambda i: (i, 0)),
        out_shape=jax.ShapeDtypeStruct((m, d), F32),
        name="ffn",
        compiler_params=_params(("parallel",), 48),
    )(x, g_ffn.reshape(1, d), w_up, w_down, g_final.reshape(1, d))


def _inproj_colscale():
    cs = jnp.ones((D_IN,), F32)
    cs = cs.at[OFF_RK:OFF_RK + RET_QK_W].set(RET_DK ** -0.5)
    cs = cs.at[OFF_DQ:OFF_DQ + DIFF_QK_W].set(DIFF_DK ** -0.5)
    return cs


def kernel(x, mem, g_mix, w_in, g_ret, w_ret_o, lambda_q1, lambda_k1, lambda_q2, lambda_k2,
           g_diff, w_diff_o, g_mem, w_mem_kv, w_cross_o, w_out, g_ffn, w_up, w_down, g_final):
    batch, seq, d = x.shape
    xf = x.reshape(batch * seq, d)
    memf = mem.reshape(batch * mem.shape[1], d)
    colscale = _inproj_colscale()
    ones_kv = jnp.ones((2 * CROSS_W,), F32)
    for l in range(DEPTH):
        lam_init = 0.8 - 0.6 * math.exp(-0.3 * l)
        z = _inproj(xf, g_mix[l], w_in[l].astype(BF16), colscale)
        ret = _retention(z, g_ret[l], batch, seq)
        da = _diff_attention(z, lambda_q1[l], lambda_k1[l], lambda_q2[l], lambda_k2[l],
                             g_diff[l], lam_init, batch, seq)
        kv = _inproj(memf, g_mem[l], w_mem_kv[l].astype(BF16), ones_kv)
        ca = _cross_attention(z, kv, batch, seq)
        xf = _merge(xf, ret, da, ca, z, w_ret_o[l].astype(BF16), w_diff_o[l].astype(BF16),
                    w_cross_o[l].astype(BF16), w_out[l].astype(BF16))
        xf = _ffn(xf, g_ffn[l], w_up[l].astype(BF16), w_down[l].astype(BF16), g_final,
                  l == DEPTH - 1)
    return xf.reshape(batch, seq, d)
```

```python
import functools
import math

import jax
import jax.numpy as jnp
from jax import lax
from jax.experimental import pallas as pl
from jax.experimental.pallas import tpu as pltpu

F32 = jnp.float32
BF16 = jnp.bfloat16

D_MODEL = 1024
DEPTH = 2
N_MEM = 256
RET_HEADS = 4
RET_DK = D_MODEL // RET_HEADS
RET_DV = 2 * RET_DK
RET_KCHUNK = 256
DIFF_HEADS = 4
DIFF_DK = D_MODEL // (2 * DIFF_HEADS)
DIFF_DV = 2 * DIFF_DK
CROSS_HEADS = 4
CROSS_DH = D_MODEL // CROSS_HEADS
D_FF = 4 * D_MODEL
EPS = 1e-6

RET_QK_W = RET_HEADS * RET_DK
RET_V_W = RET_HEADS * RET_DV
DIFF_QK_W = DIFF_HEADS * 2 * DIFF_DK
DIFF_V_W = DIFF_HEADS * DIFF_DV
CROSS_W = CROSS_HEADS * CROSS_DH
GATE_W = 3 * D_MODEL
OFF_RQ = 0
OFF_RK = OFF_RQ + RET_QK_W
OFF_RV = OFF_RK + RET_QK_W
OFF_RG = OFF_RV + RET_V_W
OFF_DQ = OFF_RG + RET_V_W
OFF_DK = OFF_DQ + DIFF_QK_W
OFF_DV = OFF_DK + DIFF_QK_W
OFF_CQ = OFF_DV + DIFF_V_W
OFF_GATE = OFF_CQ + CROSS_W
D_IN = OFF_GATE + GATE_W

MIB = 1024 * 1024
LANES = 128
POS_SPLIT = 256

INPROJ_TM = 2048
INPROJ_TN = 1024
RET_TS = 1024
DIFF_TQ = 512
DIFF_TK = 512
CROSS_TQ = 512
MERGE_TM = 512
FFN_TM = 512
FFN_TF = 1024


def _params(semantics, vmem_mib):
    return pltpu.CompilerParams(dimension_semantics=semantics,
                                vmem_limit_bytes=vmem_mib * MIB)


def _resident(shape, index_map):
    return pl.BlockSpec(shape, index_map, pipeline_mode=pl.Buffered(1))


def _rms(x, g):
    ms = jnp.mean(x * x, axis=-1, keepdims=True)
    return x * lax.rsqrt(ms + EPS) * g


def _sigmoid(x):
    return 1.0 / (1.0 + jnp.exp(-x))


def _dot(a, b):
    return jnp.dot(a, b, preferred_element_type=F32)


def _dot_nt(a, b):
    return lax.dot_general(a, b, (((1,), (1,)), ((), ())), preferred_element_type=F32)


def _dot_tn(a, b):
    return lax.dot_general(a, b, (((0,), (0,)), ((), ())), preferred_element_type=F32)


def _inproj_kernel(x_ref, g_ref, w_ref, cs_ref, o_ref, h_ref):
    @pl.when(pl.program_id(1) == 0)
    def _():
        h_ref[...] = _rms(x_ref[...], g_ref[...]).astype(BF16)

    acc = _dot(h_ref[...], w_ref[...])
    o_ref[...] = (acc * cs_ref[...]).astype(o_ref.dtype)


def _inproj(x, g, w, colscale):
    m, k = x.shape
    n = w.shape[1]
    tm = min(INPROJ_TM, m)
    tn = min(INPROJ_TN, n)
    return pl.pallas_call(
        _inproj_kernel,
        grid=(m // tm, n // tn),
        in_specs=[
            pl.BlockSpec((tm, k), lambda i, j: (i, 0)),
            pl.BlockSpec((1, k), lambda i, j: (0, 0)),
            pl.BlockSpec((k, tn), lambda i, j: (0, j)),
            pl.BlockSpec((1, tn), lambda i, j: (0, j)),
        ],
        out_specs=pl.BlockSpec((tm, tn), lambda i, j: (i, j)),
        out_shape=jax.ShapeDtypeStruct((m, n), BF16),
        scratch_shapes=[pltpu.VMEM((tm, k), BF16)],
        name="inproj",
        compiler_params=_params(("parallel", "arbitrary"), 56),
    )(x, g.reshape(1, k), w, colscale.reshape(1, n))


def _retention_kernel(chd_ref, q_ref, k_ref, v_ref, rg_ref, intra_ref, ind_ref, std_ref,
                      g_ref, o_ref, state_ref):
    h = pl.program_id(1)

    @pl.when(pl.program_id(2) == 0)
    def _():
        state_ref[...] = jnp.zeros_like(state_ref)

    ch_decay = chd_ref[h]
    gain = g_ref[...]
    c = RET_KCHUNK

    def lanes(table, width):
        return jnp.concatenate([table] * (width // LANES), axis=1)

    def chunk(ci, carry):
        rows = pl.ds(pl.multiple_of(ci * c, c), c)
        q = q_ref[rows, :]
        k = k_ref[rows, :]
        v = v_ref[rows, :]
        state = state_ref[...]
        scores = _dot_nt(q, k) * intra_ref[...]
        inner = _dot(scores.astype(BF16), v)
        cross = _dot(q, state.astype(BF16)) * lanes(ind_ref[...], RET_DV)
        kd = (k.astype(F32) * lanes(std_ref[...], RET_DK)).astype(BF16)
        state_ref[...] = ch_decay * state + _dot_tn(kd, v)
        o = inner + cross
        oc = o - jnp.mean(o, axis=-1, keepdims=True)
        y = _rms(oc, gain)
        rg = rg_ref[rows, :].astype(F32)
        o_ref[rows, :] = (rg * _sigmoid(rg) * y).astype(o_ref.dtype)
        return carry

    lax.fori_loop(0, q_ref.shape[0] // c, chunk, 0, unroll=True)


def _retention(z, g_ret, batch, seq):
    hh = RET_HEADS
    c = RET_KCHUNK
    ts = RET_TS
    nt = seq // ts
    log_g = jnp.log(1.0 - jnp.exp2(-5.0 - jnp.arange(hh, dtype=F32)))
    idx = jnp.arange(c, dtype=F32)
    rel = idx[:, None] - idx[None, :]
    intra = jnp.where(rel >= 0, jnp.exp(log_g[:, None, None] * jnp.maximum(rel, 0.0)), 0.0)
    in_decay = jnp.broadcast_to(jnp.exp(log_g[:, None] * (idx + 1.0))[:, :, None], (hh, c, LANES))
    st_decay = jnp.broadcast_to(jnp.exp(log_g[:, None] * (c - 1.0 - idx))[:, :, None],
                                (hh, c, LANES))
    ch_decay = jnp.exp(log_g * c)

    def row(b, h, t):
        return b * nt + t

    return pl.pallas_call(
        _retention_kernel,
        grid=(batch, hh, nt),
        in_specs=[
            pl.BlockSpec(memory_space=pltpu.SMEM),
            pl.BlockSpec((ts, RET_DK), lambda b, h, t: (row(b, h, t), OFF_RQ // RET_DK + h)),
            pl.BlockSpec((ts, RET_DK), lambda b, h, t: (row(b, h, t), OFF_RK // RET_DK + h)),
            pl.BlockSpec((ts, RET_DV), lambda b, h, t: (row(b, h, t), OFF_RV // RET_DV + h)),
            pl.BlockSpec((ts, RET_DV), lambda b, h, t: (row(b, h, t), OFF_RG // RET_DV + h)),
            pl.BlockSpec((None, c, c), lambda b, h, t: (h, 0, 0)),
            pl.BlockSpec((None, c, LANES), lambda b, h, t: (h, 0, 0)),
            pl.BlockSpec((None, c, LANES), lambda b, h, t: (h, 0, 0)),
            pl.BlockSpec((1, RET_DV), lambda b, h, t: (0, h)),
        ],
        out_specs=pl.BlockSpec((ts, RET_DV), lambda b, h, t: (row(b, h, t), h)),
        out_shape=jax.ShapeDtypeStruct((batch * seq, RET_V_W), BF16),
        scratch_shapes=[pltpu.VMEM((RET_DK, RET_DV), F32)],
        name="retention",
        compiler_params=_params(("parallel", "parallel", "arbitrary"), 40),
    )(ch_decay, z, z, z, z, intra, in_decay, st_decay, g_ret.reshape(1, RET_V_W))


def _diff_kernel(lam_init, slopes_ref, q_ref, k_ref, v_ref, lq1_ref, lk1_ref, lq2_ref, lk2_ref,
                 g_ref, o_ref, kaug_ref, qaug_ref, s_ref, mblk_ref, m_ref, l_ref, acc_ref):
    h = pl.program_id(1)
    qi = pl.program_id(2)
    tq = q_ref.shape[0]
    tk = DIFF_TK
    d = DIFF_DK
    slope = slopes_ref[h]

    @pl.when(qi == 0)
    def _():
        def build(j, carry):
            rows = pl.ds(pl.multiple_of(j * tk, tk), tk)
            pos = j * tk + lax.broadcasted_iota(jnp.int32, (tk, LANES), 0)
            lane = lax.broadcasted_iota(jnp.int32, (tk, LANES), 1)
            lo = pos & (POS_SPLIT - 1)
            cols = jnp.where(lane == 0, (pos - lo).astype(F32) * slope,
                             jnp.where(lane == 1, lo.astype(F32) * slope,
                                       jnp.where(lane == 2, 1.0, 0.0))).astype(BF16)
            for mp in range(2):
                kaug_ref[mp, rows, 0:d] = k_ref[rows, mp * d:(mp + 1) * d]
                kaug_ref[mp, rows, d:2 * d] = cols
            return carry

        lax.fori_loop(0, k_ref.shape[0] // tk, build, 0)

    lane = lax.broadcasted_iota(jnp.int32, (tq, LANES), 1)
    q0 = (qi * tq).astype(F32)
    qcols = jnp.where(lane < 2, 1.0, jnp.where(lane == 2, -slope * q0, 0.0)).astype(BF16)
    for mp in range(2):
        qaug_ref[mp, :, 0:d] = q_ref[:, mp * d:(mp + 1) * d]
        qaug_ref[mp, :, d:2 * d] = qcols

    m_ref[...] = jnp.full_like(m_ref, -jnp.inf)
    l_ref[...] = jnp.zeros_like(l_ref)
    acc_ref[...] = jnp.zeros_like(acc_ref)

    def scores(j, slot):
        rows = pl.ds(pl.multiple_of(j * tk, tk), tk)
        for mp in range(2):
            s = _dot_nt(kaug_ref[mp, rows, :], qaug_ref[mp])
            s_ref[slot, mp] = s
            mblk_ref[slot, mp] = jnp.max(s, axis=0, keepdims=True)

    def softmax_pv(j, slot, masked):
        rows = pl.ds(pl.multiple_of(j * tk, tk), tk)
        v = v_ref[rows, :]
        for mp in range(2):
            s = s_ref[slot, mp]
            m_old = m_ref[mp]
            if masked:
                key = lax.broadcasted_iota(jnp.int32, (tk, tq), 0)
                qry = lax.broadcasted_iota(jnp.int32, (tk, tq), 1)
                s = jnp.where(key <= qry, s, -jnp.inf)
                m_blk = jnp.max(s, axis=0, keepdims=True)
            else:
                m_blk = mblk_ref[slot, mp]
            m_new = jnp.maximum(m_old, m_blk)
            alpha = jnp.exp(m_old - m_new)
            p = jnp.exp(s - m_new)
            l_ref[mp] = alpha * l_ref[mp] + jnp.sum(p, axis=0, keepdims=True)
            m_ref[mp] = m_new
            acc_ref[mp] = alpha * acc_ref[mp] + _dot_tn(v, p.astype(BF16))

    def step(j, slot, masked, prefetch):
        if prefetch:
            scores(j + 1, 1 - slot)
        softmax_pv(j, slot, masked)

    def pair(t, carry):
        step(2 * t, 0, False, True)
        step(2 * t + 1, 1, False, True)
        return carry

    scores(0, 0)
    lax.fori_loop(0, qi // 2, pair, 0)

    @pl.when(qi % 2 == 0)
    def _():
        step(qi, 0, True, False)

    @pl.when(qi % 2 == 1)
    def _():
        step(qi - 1, 0, False, True)
        step(qi, 1, True, False)

    lam = (jnp.exp(jnp.sum(lq1_ref[...] * lk1_ref[...], keepdims=True))
           - jnp.exp(jnp.sum(lq2_ref[...] * lk2_ref[...], keepdims=True)) + lam_init)
    o = acc_ref[0] / l_ref[0] - lam * (acc_ref[1] / l_ref[1])
    o = o * lax.rsqrt(jnp.mean(o * o, axis=0, keepdims=True) + EPS)
    o_ref[...] = (o.T * g_ref[...] * (1.0 - lam_init)).astype(o_ref.dtype)


def _diff_attention(z, lq1, lk1, lq2, lk2, g_diff, lam_init, batch, seq):
    hh = DIFF_HEADS
    tq = DIFF_TQ
    assert DIFF_TQ == DIFF_TK
    assert seq <= POS_SPLIT * POS_SPLIT
    nq = seq // tq
    w = 2 * DIFF_DK
    slopes = [2.0 ** (-8.0 * (i + 1) / hh) for i in range(hh)]
    assert all(math.log2(s).is_integer() for s in slopes)
    vec = pl.BlockSpec((1, DIFF_DK), lambda b, h, i: (0, 0))
    return pl.pallas_call(
        functools.partial(_diff_kernel, lam_init),
        grid=(batch, hh, nq),
        in_specs=[
            pl.BlockSpec(memory_space=pltpu.SMEM),
            pl.BlockSpec((tq, w), lambda b, h, i: (b * nq + i, OFF_DQ // w + h)),
            pl.BlockSpec((seq, w), lambda b, h, i: (b, OFF_DK // w + h)),
            pl.BlockSpec((seq, DIFF_DV), lambda b, h, i: (b, OFF_DV // DIFF_DV + h)),
            vec, vec, vec, vec,
            pl.BlockSpec((1, DIFF_DV), lambda b, h, i: (0, h)),
        ],
        out_specs=pl.BlockSpec((tq, DIFF_DV), lambda b, h, i: (b * nq + i, h)),
        out_shape=jax.ShapeDtypeStruct((batch * seq, DIFF_V_W), BF16),
        scratch_shapes=[pltpu.VMEM((2, seq, w), BF16), pltpu.VMEM((2, tq, w), BF16),
                        pltpu.VMEM((2, 2, DIFF_TK, tq), F32),
                        pltpu.VMEM((2, 2, 1, tq), F32),
                        pltpu.VMEM((2, 1, tq), F32), pltpu.VMEM((2, 1, tq), F32),
                        pltpu.VMEM((2, DIFF_DV, tq), F32)],
        name="diff_attention",
        compiler_params=_params(("parallel", "parallel", "arbitrary"), 40),
    )(jnp.asarray(slopes, F32), z, z, z, lq1.reshape(1, -1), lk1.reshape(1, -1),
      lq2.reshape(1, -1), lk2.reshape(1, -1), g_diff.reshape(1, DIFF_V_W))


def _cross_kernel(q_ref, mk_ref, mv_ref, o_ref):
    dh = CROSS_DH
    scale = dh ** -0.5
    for h in range(CROSS_HEADS):
        cols = slice(h * dh, (h + 1) * dh)
        s = _dot_nt(q_ref[:, cols], mk_ref[:, cols]) * scale
        p = jnp.exp(s - jnp.max(s, axis=-1, keepdims=True))
        p = p / jnp.sum(p, axis=-1, keepdims=True)
        o_ref[:, cols] = _dot(p.astype(BF16), mv_ref[:, cols]).astype(o_ref.dtype)


def _cross_attention(z, kv, batch, seq):
    tq = CROSS_TQ
    nq = seq // tq
    return pl.pallas_call(
        _cross_kernel,
        grid=(batch, nq),
        in_specs=[
            pl.BlockSpec((tq, CROSS_W), lambda b, i: (b * nq + i, OFF_CQ // CROSS_W)),
            pl.BlockSpec((N_MEM, CROSS_W), lambda b, i: (b, 0)),
            pl.BlockSpec((N_MEM, CROSS_W), lambda b, i: (b, 1)),
        ],
        out_specs=pl.BlockSpec((tq, CROSS_W), lambda b, i: (b * nq + i, 0)),
        out_shape=jax.ShapeDtypeStruct((batch * seq, CROSS_W), BF16),
        name="cross_attention",
        compiler_params=_params(("parallel", "arbitrary"), 40),
    )(z, kv, kv)


def _merge_kernel(x_ref, ret_ref, da_ref, ca_ref, g0_ref, g1_ref, g2_ref,
                  wr_ref, wd_ref, wc_ref, wo_ref, o_ref):
    merged = _sigmoid(g0_ref[...].astype(F32)) * _dot(ret_ref[...], wr_ref[...])
    merged += _sigmoid(g1_ref[...].astype(F32)) * _dot(da_ref[...], wd_ref[...])
    merged += _sigmoid(g2_ref[...].astype(F32)) * _dot(ca_ref[...], wc_ref[...])
    o_ref[...] = x_ref[...] + _dot(merged.astype(BF16), wo_ref[...])


def _merge(x, ret, da, ca, z, w_ret_o, w_diff_o, w_cross_o, w_out):
    m, d = x.shape
    tm = MERGE_TM
    gate0 = OFF_GATE // d

    def rows(width):
        return pl.BlockSpec((tm, width), lambda i: (i, 0))

    def gate(n):
        return pl.BlockSpec((tm, d), lambda i: (i, gate0 + n))

    def weight(k):
        return _resident((k, d), lambda i: (0, 0))

    return pl.pallas_call(
        _merge_kernel,
        grid=(m // tm,),
        in_specs=[rows(d), rows(RET_V_W), rows(DIFF_V_W), rows(CROSS_W),
                  gate(0), gate(1), gate(2),
                  weight(RET_V_W), weight(DIFF_V_W), weight(CROSS_W), weight(d)],
        out_specs=rows(d),
        out_shape=jax.ShapeDtypeStruct((m, d), F32),
        name="merge",
        compiler_params=_params(("parallel",), 48),
    )(x, ret, da, ca, z, z, z, w_ret_o, w_diff_o, w_cross_o, w_out)


def _ffn_kernel(final_norm, x_ref, g_ref, wu_ref, wd_ref, gf_ref, o_ref):
    x = x_ref[...]
    h = _rms(x, g_ref[...]).astype(BF16)
    acc = x
    for f in range(D_FF // FFN_TF):
        cols = slice(f * FFN_TF, (f + 1) * FFN_TF)
        u = jnp.maximum(_dot(h, wu_ref[:, cols]), 0.0)
        acc = acc + _dot((u * u).astype(BF16), wd_ref[cols, :])
    if final_norm:
        acc = _rms(acc, gf_ref[...])
    o_ref[...] = acc


def _ffn(x, g_ffn, w_up, w_down, g_final, final_norm):
    m, d = x.shape
    tm = FFN_TM
    vec = pl.BlockSpec((1, d), lambda i: (0, 0))
    return pl.pallas_call(
        functools.partial(_ffn_kernel, final_norm),
        grid=(m // tm,),
        in_specs=[pl.BlockSpec((tm, d), lambda i: (i, 0)), vec,
                  _resident((d, D_FF), lambda i: (0, 0)),
                  _resident((D_FF, d), lambda i: (0, 0)), vec],
        out_specs=pl.BlockSpec((tm, d), lambda i: (i, 0)),
        out_shape=jax.ShapeDtypeStruct((m, d), F32),
        name="ffn",
        compiler_params=_params(("parallel",), 48),
    )(x, g_ffn.reshape(1, d), w_up, w_down, g_final.reshape(1, d))


def _inproj_colscale():
    cs = jnp.ones((D_IN,), F32)
    cs = cs.at[OFF_RK:OFF_RK + RET_QK_W].set(RET_DK ** -0.5)
    cs = cs.at[OFF_DQ:OFF_DQ + DIFF_QK_W].set(DIFF_DK ** -0.5)
    return cs


def kernel(x, mem, g_mix, w_in, g_ret, w_ret_o, lambda_q1, lambda_k1, lambda_q2, lambda_k2,
           g_diff, w_diff_o, g_mem, w_mem_kv, w_cross_o, w_out, g_ffn, w_up, w_down, g_final):
    batch, seq, d = x.shape
    xf = x.reshape(batch * seq, d)
    memf = mem.reshape(batch * mem.shape[1], d)
    colscale = _inproj_colscale()
    ones_kv = jnp.ones((2 * CROSS_W,), F32)
    for l in range(DEPTH):
        lam_init = 0.8 - 0.6 * math.exp(-0.3 * l)
        z = _inproj(xf, g_mix[l], w_in[l].astype(BF16), colscale)
        ret = _retention(z, g_ret[l], batch, seq)
        da = _diff_attention(z, lambda_q1[l], lambda_k1[l], lambda_q2[l], lambda_k2[l],
                             g_diff[l], lam_init, batch, seq)
        kv = _inproj(memf, g_mem[l], w_mem_kv[l].astype(BF16), ones_kv)
        ca = _cross_attention(z, kv, batch, seq)
        xf = _merge(xf, ret, da, ca, z, w_ret_o[l].astype(BF16), w_diff_o[l].astype(BF16),
                    w_cross_o[l].astype(BF16), w_out[l].astype(BF16))
        xf = _ffn(xf, g_ffn[l], w_up[l].astype(BF16), w_down[l].astype(BF16), g_final,
                  l == DEPTH - 1)
    return xf.reshape(batch, seq, d)
```

```python
import functools
import math

import jax
import jax.numpy as jnp
from jax import lax
from jax.experimental import pallas as pl
from jax.experimental.pallas import tpu as pltpu

F32 = jnp.float32
BF16 = jnp.bfloat16

D_MODEL = 1024
DEPTH = 2
N_MEM = 256
RET_HEADS = 4
RET_DK = D_MODEL // RET_HEADS
RET_DV = 2 * RET_DK
RET_KCHUNK = 256
DIFF_HEADS = 4
DIFF_DK = D_MODEL // (2 * DIFF_HEADS)
DIFF_DV = 2 * DIFF_DK
CROSS_HEADS = 4
CROSS_DH = D_MODEL // CROSS_HEADS
D_FF = 4 * D_MODEL
EPS = 1e-6
LOG2E = math.log2(math.e)

RET_QK_W = RET_HEADS * RET_DK
RET_V_W = RET_HEADS * RET_DV
DIFF_QK_W = DIFF_HEADS * 2 * DIFF_DK
DIFF_V_W = DIFF_HEADS * DIFF_DV
CROSS_W = CROSS_HEADS * CROSS_DH
GATE_W = 3 * D_MODEL
OFF_RQ = 0
OFF_RK = OFF_RQ + RET_QK_W
OFF_RV = OFF_RK + RET_QK_W
OFF_RG = OFF_RV + RET_V_W
OFF_DQ = OFF_RG + RET_V_W
OFF_DK = OFF_DQ + DIFF_QK_W
OFF_DV = OFF_DK + DIFF_QK_W
OFF_CQ = OFF_DV + DIFF_V_W
OFF_GATE = OFF_CQ + CROSS_W
D_IN = OFF_GATE + GATE_W

MIB = 1024 * 1024
LANES = 128
POS_SPLIT = 256

INPROJ_TM = 2048
INPROJ_TN = 1024
RET_TS = 1024
DIFF_TQ = 512
DIFF_TK = 512
CROSS_TQ = 512
MERGE_TM = 512
FFN_TM = 512
FFN_TF = 1024


def _params(semantics, vmem_mib):
    return pltpu.CompilerParams(dimension_semantics=semantics,
                                vmem_limit_bytes=vmem_mib * MIB)


def _resident(shape, index_map):
    return pl.BlockSpec(shape, index_map, pipeline_mode=pl.Buffered(1))


def _rms(x, g):
    ms = jnp.mean(x * x, axis=-1, keepdims=True)
    return x * lax.rsqrt(ms + EPS) * g


def _sigmoid(x):
    return 1.0 / (1.0 + jnp.exp2(x * (-LOG2E)))


def _dot(a, b):
    return jnp.dot(a, b, preferred_element_type=F32)


def _dot_nt(a, b):
    return lax.dot_general(a, b, (((1,), (1,)), ((), ())), preferred_element_type=F32)


def _dot_tn(a, b):
    return lax.dot_general(a, b, (((0,), (0,)), ((), ())), preferred_element_type=F32)


def _inproj_kernel(x_ref, g_ref, w_ref, cs_ref, o_ref, h_ref):
    @pl.when(pl.program_id(1) == 0)
    def _():
        h_ref[...] = _rms(x_ref[...], g_ref[...]).astype(BF16)

    acc = _dot(h_ref[...], w_ref[...])
    o_ref[...] = (acc * cs_ref[...]).astype(o_ref.dtype)


def _inproj(x, g, w, colscale):
    m, k = x.shape
    n = w.shape[1]
    tm = min(INPROJ_TM, m)
    tn = min(INPROJ_TN, n)
    return pl.pallas_call(
        _inproj_kernel,
        grid=(m // tm, n // tn),
        in_specs=[
            pl.BlockSpec((tm, k), lambda i, j: (i, 0)),
            pl.BlockSpec((1, k), lambda i, j: (0, 0)),
            pl.BlockSpec((k, tn), lambda i, j: (0, j)),
            pl.BlockSpec((1, tn), lambda i, j: (0, j)),
        ],
        out_specs=pl.BlockSpec((tm, tn), lambda i, j: (i, j)),
        out_shape=jax.ShapeDtypeStruct((m, n), BF16),
        scratch_shapes=[pltpu.VMEM((tm, k), BF16)],
        name="inproj",
        compiler_params=_params(("parallel", "arbitrary"), 56),
    )(x, g.reshape(1, k), w, colscale.reshape(1, n))


def _retention_kernel(chd_ref, q_ref, k_ref, v_ref, rg_ref, intra_ref, ind_ref, std_ref,
                      g_ref, o_ref, state_ref):
    h = pl.program_id(1)

    @pl.when(pl.program_id(2) == 0)
    def _():
        state_ref[...] = jnp.zeros_like(state_ref)

    ch_decay = chd_ref[h]
    gain = g_ref[...]
    c = RET_KCHUNK

    def lanes(table, width):
        return jnp.concatenate([table] * (width // LANES), axis=1)

    def chunk(ci, carry):
        rows = pl.ds(pl.multiple_of(ci * c, c), c)
        q = q_ref[rows, :]
        k = k_ref[rows, :]
        v = v_ref[rows, :]
        state = state_ref[...]
        scores = _dot_nt(q, k) * intra_ref[...]
        inner = _dot(scores.astype(BF16), v)
        qd = (q.astype(F32) * lanes(ind_ref[...], RET_DK)).astype(BF16)
        cross = _dot(qd, state.astype(BF16))
        kd = (k.astype(F32) * lanes(std_ref[...], RET_DK)).astype(BF16)
        state_ref[...] = ch_decay * state + _dot_tn(kd, v)
        o = inner + cross
        oc = o - jnp.mean(o, axis=-1, keepdims=True)
        y = _rms(oc, gain)
        rg = rg_ref[rows, :].astype(F32)
        o_ref[rows, :] = (rg * _sigmoid(rg) * y).astype(o_ref.dtype)
        return carry

    lax.fori_loop(0, q_ref.shape[0] // c, chunk, 0, unroll=True)


def _retention(z, g_ret, batch, seq):
    hh = RET_HEADS
    c = RET_KCHUNK
    ts = RET_TS
    nt = seq // ts
    log_g = jnp.log(1.0 - jnp.exp2(-5.0 - jnp.arange(hh, dtype=F32)))
    idx = jnp.arange(c, dtype=F32)
    rel = idx[:, None] - idx[None, :]
    intra = jnp.where(rel >= 0, jnp.exp(log_g[:, None, None] * jnp.maximum(rel, 0.0)), 0.0)
    in_decay = jnp.broadcast_to(jnp.exp(log_g[:, None] * (idx + 1.0))[:, :, None], (hh, c, LANES))
    st_decay = jnp.broadcast_to(jnp.exp(log_g[:, None] * (c - 1.0 - idx))[:, :, None],
                                (hh, c, LANES))
    ch_decay = jnp.exp(log_g * c)

    def row(b, h, t):
        return b * nt + t

    return pl.pallas_call(
        _retention_kernel,
        grid=(batch, hh, nt),
        in_specs=[
            pl.BlockSpec(memory_space=pltpu.SMEM),
            pl.BlockSpec((ts, RET_DK), lambda b, h, t: (row(b, h, t), OFF_RQ // RET_DK + h)),
            pl.BlockSpec((ts, RET_DK), lambda b, h, t: (row(b, h, t), OFF_RK // RET_DK + h)),
            pl.BlockSpec((ts, RET_DV), lambda b, h, t: (row(b, h, t), OFF_RV // RET_DV + h)),
            pl.BlockSpec((ts, RET_DV), lambda b, h, t: (row(b, h, t), OFF_RG // RET_DV + h)),
            pl.BlockSpec((None, c, c), lambda b, h, t: (h, 0, 0)),
            pl.BlockSpec((None, c, LANES), lambda b, h, t: (h, 0, 0)),
            pl.BlockSpec((None, c, LANES), lambda b, h, t: (h, 0, 0)),
            pl.BlockSpec((1, RET_DV), lambda b, h, t: (0, h)),
        ],
        out_specs=pl.BlockSpec((ts, RET_DV), lambda b, h, t: (row(b, h, t), h)),
        out_shape=jax.ShapeDtypeStruct((batch * seq, RET_V_W), BF16),
        scratch_shapes=[pltpu.VMEM((RET_DK, RET_DV), F32)],
        name="retention",
        compiler_params=_params(("parallel", "parallel", "arbitrary"), 40),
    )(ch_decay, z, z, z, z, intra, in_decay, st_decay, g_ret.reshape(1, RET_V_W))


def _diff_kernel(lam_init, slopes_ref, q_ref, k_ref, v_ref, lq1_ref, lk1_ref, lq2_ref, lk2_ref,
                 g_ref, o_ref, kaug_ref, qaug_ref, s_ref, mblk_ref, m_ref, l_ref, acc_ref):
    h = pl.program_id(1)
    seq = q_ref.shape[0]
    tq = DIFF_TQ
    tk = DIFF_TK
    d = DIFF_DK
    nq = seq // tq
    slope = slopes_ref[h]

    def build(j, carry):
        rows = pl.ds(pl.multiple_of(j * tk, tk), tk)
        pos = j * tk + lax.broadcasted_iota(jnp.int32, (tk, LANES), 0)
        lane = lax.broadcasted_iota(jnp.int32, (tk, LANES), 1)
        lo = pos & (POS_SPLIT - 1)
        kcols = jnp.where(lane == 0, (pos - lo).astype(F32) * slope,
                          jnp.where(lane == 1, lo.astype(F32) * slope,
                                    jnp.where(lane == 2, 1.0, 0.0))).astype(BF16)
        q0 = (pos - (pos & (tq - 1))).astype(F32)
        qcols = jnp.where(lane < 2, 1.0, jnp.where(lane == 2, -slope * q0, 0.0)).astype(BF16)
        for mp in range(2):
            kaug_ref[mp, rows, 0:d] = k_ref[rows, mp * d:(mp + 1) * d]
            kaug_ref[mp, rows, d:2 * d] = kcols
            qaug_ref[mp, rows, 0:d] = q_ref[rows, mp * d:(mp + 1) * d]
            qaug_ref[mp, rows, d:2 * d] = qcols
        return carry

    lax.fori_loop(0, seq // tk, build, 0)

    lam = (jnp.exp(jnp.sum(lq1_ref[...] * lk1_ref[...], keepdims=True))
           - jnp.exp(jnp.sum(lq2_ref[...] * lk2_ref[...], keepdims=True)) + lam_init)

    def scores(qi, j, slot):
        krows = pl.ds(pl.multiple_of(j * tk, tk), tk)
        qrows = pl.ds(pl.multiple_of(qi * tq, tq), tq)
        for mp in range(2):
            s = _dot_nt(kaug_ref[mp, krows, :], qaug_ref[mp, qrows, :])
            s_ref[slot, mp] = s
            mblk_ref[slot, mp] = jnp.max(s, axis=0, keepdims=True)

    def softmax_pv(j, slot, masked):
        rows = pl.ds(pl.multiple_of(j * tk, tk), tk)
        v = v_ref[rows, :]
        for mp in range(2):
            s = s_ref[slot, mp]
            m_old = m_ref[mp]
            if masked:
                key = lax.broadcasted_iota(jnp.int32, (tk, tq), 0)
                qry = lax.broadcasted_iota(jnp.int32, (tk, tq), 1)
                s = jnp.where(key <= qry, s, -jnp.inf)
                m_blk = jnp.max(s, axis=0, keepdims=True)
            else:
                m_blk = mblk_ref[slot, mp]
            m_new = jnp.maximum(m_old, m_blk)
            alpha = jnp.exp(m_old - m_new)
            p = jnp.exp(s - m_new)
            l_ref[mp] = alpha * l_ref[mp] + jnp.sum(p, axis=0, keepdims=True)
            m_ref[mp] = m_new
            acc_ref[mp] = alpha * acc_ref[mp] + _dot_tn(v, p.astype(BF16))

    def step(qi, j, slot, masked, prefetch):
        if prefetch:
            scores(qi, j + 1, 1 - slot)
        softmax_pv(j, slot, masked)

    def tile(qi, carry):
        m_ref[...] = jnp.full_like(m_ref, -jnp.inf)
        l_ref[...] = jnp.zeros_like(l_ref)
        acc_ref[...] = jnp.zeros_like(acc_ref)

        def pair(t, c):
            step(qi, 2 * t, 0, False, True)
            step(qi, 2 * t + 1, 1, False, True)
            return c

        lax.fori_loop(0, qi // 2, pair, 0)

        @pl.when(qi % 2 == 0)
        def _():
            step(qi, qi, 0, True, False)

        @pl.when(qi % 2 == 1)
        def _():
            step(qi, qi - 1, 0, False, True)
            step(qi, qi, 1, True, False)

        scores(jnp.minimum(qi + 1, nq - 1), 0, 0)
        o = acc_ref[0] / l_ref[0] - lam * (acc_ref[1] / l_ref[1])
        o = o * lax.rsqrt(jnp.mean(o * o, axis=0, keepdims=True) + EPS)
        qrows = pl.ds(pl.multiple_of(qi * tq, tq), tq)
        o_ref[qrows, :] = (o.T * g_ref[...] * (1.0 - lam_init)).astype(o_ref.dtype)
        return carry

    scores(0, 0, 0)
    lax.fori_loop(0, nq, tile, 0)


def _diff_attention(z, lq1, lk1, lq2, lk2, g_diff, lam_init, batch, seq):
    hh = DIFF_HEADS
    tq = DIFF_TQ
    assert DIFF_TQ == DIFF_TK and tq & (tq - 1) == 0 and seq % tq == 0
    assert seq <= POS_SPLIT * POS_SPLIT
    w = 2 * DIFF_DK
    slopes = [2.0 ** (-8.0 * (i + 1) / hh) for i in range(hh)]
    assert all(math.log2(s).is_integer() for s in slopes)
    vec = pl.BlockSpec((1, DIFF_DK), lambda b, h: (0, 0))
    return pl.pallas_call(
        functools.partial(_diff_kernel, lam_init),
        grid=(batch, hh),
        in_specs=[
            pl.BlockSpec(memory_space=pltpu.SMEM),
            pl.BlockSpec((seq, w), lambda b, h: (b, OFF_DQ // w + h)),
            pl.BlockSpec((seq, w), lambda b, h: (b, OFF_DK // w + h)),
            pl.BlockSpec((seq, DIFF_DV), lambda b, h: (b, OFF_DV // DIFF_DV + h)),
            vec, vec, vec, vec,
            pl.BlockSpec((1, DIFF_DV), lambda b, h: (0, h)),
        ],
        out_specs=pl.BlockSpec((seq, DIFF_DV), lambda b, h: (b, h)),
        out_shape=jax.ShapeDtypeStruct((batch * seq, DIFF_V_W), BF16),
        scratch_shapes=[pltpu.VMEM((2, seq, w), BF16), pltpu.VMEM((2, seq, w), BF16),
                        pltpu.VMEM((2, 2, DIFF_TK, tq), F32),
                        pltpu.VMEM((2, 2, 1, tq), F32),
                        pltpu.VMEM((2, 1, tq), F32), pltpu.VMEM((2, 1, tq), F32),
                        pltpu.VMEM((2, DIFF_DV, tq), F32)],
        name="diff_attention",
        compiler_params=_params(("parallel", "parallel"), 48),
    )(jnp.asarray(slopes, F32), z, z, z, lq1.reshape(1, -1), lk1.reshape(1, -1),
      lq2.reshape(1, -1), lk2.reshape(1, -1), g_diff.reshape(1, DIFF_V_W))


def _cross_kernel(q_ref, mk_ref, mv_ref, o_ref):
    dh = CROSS_DH
    scale = dh ** -0.5
    for h in range(CROSS_HEADS):
        cols = slice(h * dh, (h + 1) * dh)
        s = _dot_nt(q_ref[:, cols], mk_ref[:, cols]) * scale
        p = jnp.exp(s - jnp.max(s, axis=-1, keepdims=True))
        p = p / jnp.sum(p, axis=-1, keepdims=True)
        o_ref[:, cols] = _dot(p.astype(BF16), mv_ref[:, cols]).astype(o_ref.dtype)


def _cross_attention(z, kv, batch, seq):
    tq = CROSS_TQ
    nq = seq // tq
    return pl.pallas_call(
        _cross_kernel,
        grid=(batch, nq),
        in_specs=[
            pl.BlockSpec((tq, CROSS_W), lambda b, i: (b * nq + i, OFF_CQ // CROSS_W)),
            pl.BlockSpec((N_MEM, CROSS_W), lambda b, i: (b, 0)),
            pl.BlockSpec((N_MEM, CROSS_W), lambda b, i: (b, 1)),
        ],
        out_specs=pl.BlockSpec((tq, CROSS_W), lambda b, i: (b * nq + i, 0)),
        out_shape=jax.ShapeDtypeStruct((batch * seq, CROSS_W), BF16),
        name="cross_attention",
        compiler_params=_params(("parallel", "arbitrary"), 40),
    )(z, kv, kv)


def _merge_kernel(x_ref, ret_ref, da_ref, ca_ref, g0_ref, g1_ref, g2_ref,
                  wr_ref, wd_ref, wc_ref, wo_ref, o_ref):
    merged = _sigmoid(g0_ref[...].astype(F32)) * _dot(ret_ref[...], wr_ref[...])
    merged += _sigmoid(g1_ref[...].astype(F32)) * _dot(da_ref[...], wd_ref[...])
    merged += _sigmoid(g2_ref[...].astype(F32)) * _dot(ca_ref[...], wc_ref[...])
    o_ref[...] = x_ref[...] + _dot(merged.astype(BF16), wo_ref[...])


def _merge(x, ret, da, ca, z, w_ret_o, w_diff_o, w_cross_o, w_out):
    m, d = x.shape
    tm = MERGE_TM
    gate0 = OFF_GATE // d

    def rows(width):
        return pl.BlockSpec((tm, width), lambda i: (i, 0))

    def gate(n):
        return pl.BlockSpec((tm, d), lambda i: (i, gate0 + n))

    def weight(k):
        return _resident((k, d), lambda i: (0, 0))

    return pl.pallas_call(
        _merge_kernel,
        grid=(m // tm,),
        in_specs=[rows(d), rows(RET_V_W), rows(DIFF_V_W), rows(CROSS_W),
                  gate(0), gate(1), gate(2),
                  weight(RET_V_W), weight(DIFF_V_W), weight(CROSS_W), weight(d)],
        out_specs=rows(d),
        out_shape=jax.ShapeDtypeStruct((m, d), F32),
        name="merge",
        compiler_params=_params(("parallel",), 48),
    )(x, ret, da, ca, z, z, z, w_ret_o, w_diff_o, w_cross_o, w_out)


def _ffn_kernel(final_norm, x_ref, g_ref, wu_ref, wd_ref, gf_ref, o_ref):
    x = x_ref[...]
    h = _rms(x, g_ref[...]).astype(BF16)
    acc = x
    for f in range(D_FF // FFN_TF):
        cols = slice(f * FFN_TF, (f + 1) * FFN_TF)
        u = jnp.maximum(_dot(h, wu_ref[:, cols]), 0.0)
        acc = acc + _dot((u * u).astype(BF16), wd_ref[cols, :])
    if final_norm:
        acc = _rms(acc, gf_ref[...])
    o_ref[...] = acc


def _ffn(x, g_ffn, w_up, w_down, g_final, final_norm):
    m, d = x.shape
    tm = FFN_TM
    vec = pl.BlockSpec((1, d), lambda i: (0, 0))
    return pl.pallas_call(
        functools.partial(_ffn_kernel, final_norm),
        grid=(m // tm,),
        in_specs=[pl.BlockSpec((tm, d), lambda i: (i, 0)), vec,
                  _resident((d, D_FF), lambda i: (0, 0)),
                  _resident((D_FF, d), lambda i: (0, 0)), vec],
        out_specs=pl.BlockSpec((tm, d), lambda i: (i, 0)),
        out_shape=jax.ShapeDtypeStruct((m, d), F32),
        name="ffn",
        compiler_params=_params(("parallel",), 48),
    )(x, g_ffn.reshape(1, d), w_up, w_down, g_final.reshape(1, d))


def _inproj_colscale():
    cs = jnp.ones((D_IN,), F32)
    cs = cs.at[OFF_RK:OFF_RK + RET_QK_W].set(RET_DK ** -0.5)
    cs = cs.at[OFF_DQ:OFF_DQ + DIFF_QK_W].set(DIFF_DK ** -0.5)
    return cs


def kernel(x, mem, g_mix, w_in, g_ret, w_ret_o, lambda_q1, lambda_k1, lambda_q2, lambda_k2,
           g_diff, w_diff_o, g_mem, w_mem_kv, w_cross_o, w_out, g_ffn, w_up, w_down, g_final):
    batch, seq, d = x.shape
    xf = x.reshape(batch * seq, d)
    memf = mem.reshape(batch * mem.shape[1], d)
    colscale = _inproj_colscale()
    ones_kv = jnp.ones((2 * CROSS_W,), F32)
    for l in range(DEPTH):
        lam_init = 0.8 - 0.6 * math.exp(-0.3 * l)
        z = _inproj(xf, g_mix[l], w_in[l].astype(BF16), colscale)
        ret = _retention(z, g_ret[l], batch, seq)
        da = _diff_attention(z, lambda_q1[l], lambda_k1[l], lambda_q2[l], lambda_k2[l],
                             g_diff[l], lam_init, batch, seq)
        kv = _inproj(memf, g_mem[l], w_mem_kv[l].astype(BF16), ones_kv)
        ca = _cross_attention(z, kv, batch, seq)
        xf = _merge(xf, ret, da, ca, z, w_ret_o[l].astype(BF16), w_diff_o[l].astype(BF16),
                    w_cross_o[l].astype(BF16), w_out[l].astype(BF16))
        xf = _ffn(xf, g_ffn[l], w_up[l].astype(BF16), w_down[l].astype(BF16), g_final,
                  l == DEPTH - 1)
    return xf.reshape(batch, seq, d)
```

```python
import functools
import math

import jax
import jax.numpy as jnp
from jax import lax
from jax.experimental import pallas as pl
from jax.experimental.pallas import tpu as pltpu

F32 = jnp.float32
BF16 = jnp.bfloat16

D_MODEL = 1024
DEPTH = 2
N_MEM = 256
RET_HEADS = 4
RET_DK = D_MODEL // RET_HEADS
RET_DV = 2 * RET_DK
RET_KCHUNK = 256
DIFF_HEADS = 4
DIFF_DK = D_MODEL // (2 * DIFF_HEADS)
DIFF_DV = 2 * DIFF_DK
CROSS_HEADS = 4
CROSS_DH = D_MODEL // CROSS_HEADS
D_FF = 4 * D_MODEL
EPS = 1e-6
LOG2E = math.log2(math.e)

RET_QK_W = RET_HEADS * RET_DK
RET_V_W = RET_HEADS * RET_DV
DIFF_QK_W = DIFF_HEADS * 2 * DIFF_DK
DIFF_V_W = DIFF_HEADS * DIFF_DV
CROSS_W = CROSS_HEADS * CROSS_DH
GATE_W = 3 * D_MODEL
OFF_RQ = 0
OFF_RK = OFF_RQ + RET_QK_W
OFF_RV = OFF_RK + RET_QK_W
OFF_RG = OFF_RV + RET_V_W
OFF_DQ = OFF_RG + RET_V_W
OFF_DK = OFF_DQ + DIFF_QK_W
OFF_DV = OFF_DK + DIFF_QK_W
OFF_CQ = OFF_DV + DIFF_V_W
OFF_GATE = OFF_CQ + CROSS_W
D_IN = OFF_GATE + GATE_W

MIB = 1024 * 1024
LANES = 128
POS_SPLIT = 256

INPROJ_TM = 2048
INPROJ_TN = 1024
RET_TS = 1024
DIFF_TQ = 512
DIFF_TK = 512
CROSS_TQ = 1024
MERGE_TM = 512
FFN_TM = 512
FFN_TF = 1024


def _params(semantics, vmem_mib):
    return pltpu.CompilerParams(dimension_semantics=semantics,
                                vmem_limit_bytes=vmem_mib * MIB)


def _resident(shape, index_map):
    return pl.BlockSpec(shape, index_map, pipeline_mode=pl.Buffered(1))


def _rms(x, g):
    ms = jnp.mean(x * x, axis=-1, keepdims=True)
    return x * lax.rsqrt(ms + EPS) * g


def _sigmoid(x):
    return 1.0 / (1.0 + jnp.exp2(x * (-LOG2E)))


def _dot(a, b):
    return jnp.dot(a, b, preferred_element_type=F32)


def _dot_nt(a, b):
    return lax.dot_general(a, b, (((1,), (1,)), ((), ())), preferred_element_type=F32)


def _dot_tn(a, b):
    return lax.dot_general(a, b, (((0,), (0,)), ((), ())), preferred_element_type=F32)


def _inproj_kernel(x_ref, g_ref, w_ref, cs_ref, o_ref, h_ref):
    @pl.when(pl.program_id(1) == 0)
    def _():
        h_ref[...] = _rms(x_ref[...], g_ref[...]).astype(BF16)

    acc = _dot(h_ref[...], w_ref[...])
    o_ref[...] = (acc * cs_ref[...]).astype(o_ref.dtype)


def _inproj(x, g, w, layer, colscale):
    m, k = x.shape
    n = w.shape[2]
    tm = min(INPROJ_TM, m)
    tn = min(INPROJ_TN, n)
    return pl.pallas_call(
        _inproj_kernel,
        grid=(m // tm, n // tn),
        in_specs=[
            pl.BlockSpec((tm, k), lambda i, j: (i, 0)),
            pl.BlockSpec((1, k), lambda i, j: (0, 0)),
            pl.BlockSpec((None, k, tn), lambda i, j: (layer, 0, j)),
            pl.BlockSpec((1, tn), lambda i, j: (0, j)),
        ],
        out_specs=pl.BlockSpec((tm, tn), lambda i, j: (i, j)),
        out_shape=jax.ShapeDtypeStruct((m, n), BF16),
        scratch_shapes=[pltpu.VMEM((tm, k), BF16)],
        name="inproj",
        compiler_params=_params(("parallel", "arbitrary"), 56),
    )(x, g.reshape(1, k), w, colscale.reshape(1, n))


def _retention_kernel(chd_ref, q_ref, k_ref, v_ref, rg_ref, intra_ref, ind_ref, std_ref,
                      g_ref, o_ref, state_ref):
    h = pl.program_id(1)

    @pl.when(pl.program_id(2) == 0)
    def _():
        state_ref[...] = jnp.zeros_like(state_ref)

    ch_decay = chd_ref[h]
    gain = g_ref[...]
    c = RET_KCHUNK

    def lanes(table, width):
        return jnp.concatenate([table] * (width // LANES), axis=1)

    def chunk(ci, carry):
        rows = pl.ds(pl.multiple_of(ci * c, c), c)
        q = q_ref[rows, :]
        k = k_ref[rows, :]
        v = v_ref[rows, :]
        state = state_ref[...]
        scores = _dot_nt(q, k) * intra_ref[...]
        inner = _dot(scores.astype(BF16), v)
        qd = (q.astype(F32) * lanes(ind_ref[...], RET_DK)).astype(BF16)
        cross = _dot(qd, state.astype(BF16))
        kd = (k.astype(F32) * lanes(std_ref[...], RET_DK)).astype(BF16)
        state_ref[...] = ch_decay * state + _dot_tn(kd, v)
        o = inner + cross
        oc = o - jnp.mean(o, axis=-1, keepdims=True)
        y = _rms(oc, gain)
        rg = rg_ref[rows, :].astype(F32)
        o_ref[rows, :] = (rg * _sigmoid(rg) * y).astype(o_ref.dtype)
        return carry

    lax.fori_loop(0, q_ref.shape[0] // c, chunk, 0, unroll=True)


def _retention(z, g_ret, batch, seq):
    hh = RET_HEADS
    c = RET_KCHUNK
    ts = RET_TS
    nt = seq // ts
    log_g = jnp.log(1.0 - jnp.exp2(-5.0 - jnp.arange(hh, dtype=F32)))
    idx = jnp.arange(c, dtype=F32)
    rel = idx[:, None] - idx[None, :]
    intra = jnp.where(rel >= 0, jnp.exp(log_g[:, None, None] * jnp.maximum(rel, 0.0)), 0.0)
    in_decay = jnp.broadcast_to(jnp.exp(log_g[:, None] * (idx + 1.0))[:, :, None], (hh, c, LANES))
    st_decay = jnp.broadcast_to(jnp.exp(log_g[:, None] * (c - 1.0 - idx))[:, :, None],
                                (hh, c, LANES))
    ch_decay = jnp.exp(log_g * c)

    def row(b, h, t):
        return b * nt + t

    return pl.pallas_call(
        _retention_kernel,
        grid=(batch, hh, nt),
        in_specs=[
            pl.BlockSpec(memory_space=pltpu.SMEM),
            pl.BlockSpec((ts, RET_DK), lambda b, h, t: (row(b, h, t), OFF_RQ // RET_DK + h)),
            pl.BlockSpec((ts, RET_DK), lambda b, h, t: (row(b, h, t), OFF_RK // RET_DK + h)),
            pl.BlockSpec((ts, RET_DV), lambda b, h, t: (row(b, h, t), OFF_RV // RET_DV + h)),
            pl.BlockSpec((ts, RET_DV), lambda b, h, t: (row(b, h, t), OFF_RG // RET_DV + h)),
            pl.BlockSpec((None, c, c), lambda b, h, t: (h, 0, 0)),
            pl.BlockSpec((None, c, LANES), lambda b, h, t: (h, 0, 0)),
            pl.BlockSpec((None, c, LANES), lambda b, h, t: (h, 0, 0)),
            pl.BlockSpec((1, RET_DV), lambda b, h, t: (0, h)),
        ],
        out_specs=pl.BlockSpec((ts, RET_DV), lambda b, h, t: (row(b, h, t), h)),
        out_shape=jax.ShapeDtypeStruct((batch * seq, RET_V_W), BF16),
        scratch_shapes=[pltpu.VMEM((RET_DK, RET_DV), F32)],
        name="retention",
        compiler_params=_params(("parallel", "parallel", "arbitrary"), 40),
    )(ch_decay, z, z, z, z, intra, in_decay, st_decay, g_ret.reshape(1, RET_V_W))


def _diff_kernel(lam_init, slopes_ref, q_ref, k_ref, v_ref, lq1_ref, lk1_ref, lq2_ref, lk2_ref,
                 g_ref, o_ref, kaug_ref, qaug_ref, s_ref, mblk_ref, m_ref, l_ref, acc_ref):
    h = pl.program_id(1)
    seq = q_ref.shape[0]
    tq = DIFF_TQ
    tk = DIFF_TK
    d = DIFF_DK
    nq = seq // tq
    slope = slopes_ref[h]

    def build(j, carry):
        rows = pl.ds(pl.multiple_of(j * tk, tk), tk)
        pos = j * tk + lax.broadcasted_iota(jnp.int32, (tk, LANES), 0)
        lane = lax.broadcasted_iota(jnp.int32, (tk, LANES), 1)
        lo = pos & (POS_SPLIT - 1)
        kcols = jnp.where(lane == 0, (pos - lo).astype(F32) * slope,
                          jnp.where(lane == 1, lo.astype(F32) * slope,
                                    jnp.where(lane == 2, 1.0, 0.0))).astype(BF16)
        q0 = (pos - (pos & (tq - 1))).astype(F32)
        qcols = jnp.where(lane < 2, 1.0, jnp.where(lane == 2, -slope * q0, 0.0)).astype(BF16)
        for mp in range(2):
            kaug_ref[mp, rows, 0:d] = k_ref[rows, mp * d:(mp + 1) * d]
            kaug_ref[mp, rows, d:2 * d] = kcols
            qaug_ref[mp, rows, 0:d] = q_ref[rows, mp * d:(mp + 1) * d]
            qaug_ref[mp, rows, d:2 * d] = qcols
        return carry

    lax.fori_loop(0, seq // tk, build, 0)

    lam = (jnp.exp(jnp.sum(lq1_ref[...] * lk1_ref[...], keepdims=True))
           - jnp.exp(jnp.sum(lq2_ref[...] * lk2_ref[...], keepdims=True)) + lam_init)

    def scores(qi, j, nblk, slot):
        krows = pl.ds(pl.multiple_of(j * tk, tk), nblk * tk)
        qrows = pl.ds(pl.multiple_of(qi * tq, tq), tq)
        for mp in range(2):
            s = _dot_nt(kaug_ref[mp, krows, :], qaug_ref[mp, qrows, :])
            s_ref[slot, mp, 0:nblk * tk] = s
            mblk_ref[slot, mp] = jnp.max(s, axis=0, keepdims=True)

    def softmax_pv(j, nblk, slot, diag):
        nk = nblk * tk
        rows = pl.ds(pl.multiple_of(j * tk, tk), nk)
        v = v_ref[rows, :]
        for mp in range(2):
            s = s_ref[slot, mp, 0:nk]
            m_old = m_ref[mp]
            if diag:
                key = lax.broadcasted_iota(jnp.int32, (nk, tq), 0) - (nk - tk)
                qry = lax.broadcasted_iota(jnp.int32, (nk, tq), 1)
                s = jnp.where(key <= qry, s, -jnp.inf)
                m_blk = jnp.max(s, axis=0, keepdims=True)
            else:
                m_blk = mblk_ref[slot, mp]
            m_new = jnp.maximum(m_old, m_blk)
            alpha = jnp.exp(m_old - m_new)
            p = jnp.exp(s - m_new)
            l_ref[mp] = alpha * l_ref[mp] + jnp.sum(p, axis=0, keepdims=True)
            m_ref[mp] = m_new
            acc_ref[mp] = alpha * acc_ref[mp] + _dot_tn(v, p.astype(BF16))

    def step(qi, j, nblk, slot, diag, following):
        if following is not None:
            scores(qi, following[0], following[1], 1 - slot)
        softmax_pv(j, nblk, slot, diag)

    def tile(qi, carry):
        m_ref[...] = jnp.full_like(m_ref, -jnp.inf)
        l_ref[...] = jnp.zeros_like(l_ref)
        acc_ref[...] = jnp.zeros_like(acc_ref)

        def pair(t, c):
            step(qi, 4 * t, 2, 0, False, (4 * t + 2, 2))
            step(qi, 4 * t + 2, 2, 1, False, (4 * t + 4, 2))
            return c

        lax.fori_loop(0, qi // 4, pair, 0)
        rest = qi % 4

        @pl.when(rest == 0)
        def _():
            step(qi, qi, 1, 0, True, None)

        @pl.when(rest == 1)
        def _():
            step(qi, qi - 1, 2, 0, True, None)

        @pl.when(rest == 2)
        def _():
            step(qi, qi - 2, 2, 0, False, (qi, 1))
            step(qi, qi, 1, 1, True, None)

        @pl.when(rest == 3)
        def _():
            step(qi, qi - 3, 2, 0, False, (qi - 1, 2))
            step(qi, qi - 1, 2, 1, True, None)

        scores(jnp.minimum(qi + 1, nq - 1), 0, 2, 0)
        o = acc_ref[0] / l_ref[0] - lam * (acc_ref[1] / l_ref[1])
        o = o * lax.rsqrt(jnp.mean(o * o, axis=0, keepdims=True) + EPS)
        qrows = pl.ds(pl.multiple_of(qi * tq, tq), tq)
        o_ref[qrows, :] = (o.T * g_ref[...] * (1.0 - lam_init)).astype(o_ref.dtype)
        return carry

    scores(0, 0, 1, 0)
    lax.fori_loop(0, nq, tile, 0)


def _diff_attention(z, lq1, lk1, lq2, lk2, g_diff, lam_init, batch, seq):
    hh = DIFF_HEADS
    tq = DIFF_TQ
    assert DIFF_TQ == DIFF_TK and tq & (tq - 1) == 0 and seq % tq == 0
    assert seq <= POS_SPLIT * POS_SPLIT
    w = 2 * DIFF_DK
    slopes = [2.0 ** (-8.0 * (i + 1) / hh) for i in range(hh)]
    assert all(math.log2(s).is_integer() for s in slopes)
    vec = pl.BlockSpec((1, DIFF_DK), lambda b, h: (0, 0))
    return pl.pallas_call(
        functools.partial(_diff_kernel, lam_init),
        grid=(batch, hh),
        in_specs=[
            pl.BlockSpec(memory_space=pltpu.SMEM),
            pl.BlockSpec((seq, w), lambda b, h: (b, OFF_DQ // w + h)),
            pl.BlockSpec((seq, w), lambda b, h: (b, OFF_DK // w + h)),
            pl.BlockSpec((seq, DIFF_DV), lambda b, h: (b, OFF_DV // DIFF_DV + h)),
            vec, vec, vec, vec,
            pl.BlockSpec((1, DIFF_DV), lambda b, h: (0, h)),
        ],
        out_specs=pl.BlockSpec((seq, DIFF_DV), lambda b, h: (b, h)),
        out_shape=jax.ShapeDtypeStruct((batch * seq, DIFF_V_W), BF16),
        scratch_shapes=[pltpu.VMEM((2, seq, w), BF16), pltpu.VMEM((2, seq, w), BF16),
                        pltpu.VMEM((2, 2, 2 * DIFF_TK, tq), F32),
                        pltpu.VMEM((2, 2, 1, tq), F32),
                        pltpu.VMEM((2, 1, tq), F32), pltpu.VMEM((2, 1, tq), F32),
                        pltpu.VMEM((2, DIFF_DV, tq), F32)],
        name="diff_attention",
        compiler_params=_params(("parallel", "parallel"), 48),
    )(jnp.asarray(slopes, F32), z, z, z, lq1.reshape(1, -1), lk1.reshape(1, -1),
      lq2.reshape(1, -1), lk2.reshape(1, -1), g_diff.reshape(1, DIFF_V_W))


def _cross_kernel(q_ref, mk_ref, mv_ref, o_ref):
    dh = CROSS_DH
    for h in range(CROSS_HEADS):
        cols = slice(h * dh, (h + 1) * dh)
        s = _dot_nt(mk_ref[:, cols], q_ref[:, cols])
        p = jnp.exp(s - jnp.max(s, axis=0, keepdims=True))
        p = p * (1.0 / jnp.sum(p, axis=0, keepdims=True))
        o_ref[:, cols] = _dot_tn(p.astype(BF16), mv_ref[:, cols]).astype(o_ref.dtype)


def _cross_attention(z, kv, batch, seq):
    tq = CROSS_TQ
    nq = seq // tq
    return pl.pallas_call(
        _cross_kernel,
        grid=(batch, nq),
        in_specs=[
            pl.BlockSpec((tq, CROSS_W), lambda b, i: (b * nq + i, OFF_CQ // CROSS_W)),
            pl.BlockSpec((N_MEM, CROSS_W), lambda b, i: (b, 0)),
            pl.BlockSpec((N_MEM, CROSS_W), lambda b, i: (b, 1)),
        ],
        out_specs=pl.BlockSpec((tq, CROSS_W), lambda b, i: (b * nq + i, 0)),
        out_shape=jax.ShapeDtypeStruct((batch * seq, CROSS_W), BF16),
        name="cross_attention",
        compiler_params=_params(("parallel", "arbitrary"), 40),
    )(z, kv, kv)


def _merge_kernel(x_ref, ret_ref, da_ref, ca_ref, g0_ref, g1_ref, g2_ref,
                  wr_ref, wd_ref, wc_ref, wo_ref, o_ref):
    merged = _sigmoid(g0_ref[...].astype(F32)) * _dot(ret_ref[...], wr_ref[...])
    merged += _sigmoid(g1_ref[...].astype(F32)) * _dot(da_ref[...], wd_ref[...])
    merged += _sigmoid(g2_ref[...].astype(F32)) * _dot(ca_ref[...], wc_ref[...])
    o_ref[...] = x_ref[...] + _dot(merged.astype(BF16), wo_ref[...])


def _merge(x, ret, da, ca, z, layer, w_ret_o, w_diff_o, w_cross_o, w_out):
    m, d = x.shape
    tm = MERGE_TM
    gate0 = OFF_GATE // d

    def rows(width):
        return pl.BlockSpec((tm, width), lambda i: (i, 0))

    def gate(n):
        return pl.BlockSpec((tm, d), lambda i: (i, gate0 + n))

    def weight(k):
        return _resident((None, k, d), lambda i: (layer, 0, 0))

    return pl.pallas_call(
        _merge_kernel,
        grid=(m // tm,),
        in_specs=[rows(d), rows(RET_V_W), rows(DIFF_V_W), rows(CROSS_W),
                  gate(0), gate(1), gate(2),
                  weight(RET_V_W), weight(DIFF_V_W), weight(CROSS_W), weight(d)],
        out_specs=rows(d),
        out_shape=jax.ShapeDtypeStruct((m, d), F32),
        name="merge",
        compiler_params=_params(("parallel",), 48),
    )(x, ret, da, ca, z, z, z, w_ret_o, w_diff_o, w_cross_o, w_out)


def _ffn_kernel(final_norm, x_ref, g_ref, wu_ref, wd_ref, gf_ref, o_ref):
    x = x_ref[...]
    h = _rms(x, g_ref[...]).astype(BF16)
    acc = x
    for f in range(D_FF // FFN_TF):
        cols = slice(f * FFN_TF, (f + 1) * FFN_TF)
        u = jnp.maximum(_dot(h, wu_ref[:, cols]), 0.0)
        acc = acc + _dot((u * u).astype(BF16), wd_ref[cols, :])
    if final_norm:
        acc = _rms(acc, gf_ref[...])
    o_ref[...] = acc


def _ffn(x, g_ffn, layer, w_up, w_down, g_final, final_norm):
    m, d = x.shape
    tm = FFN_TM
    vec = pl.BlockSpec((1, d), lambda i: (0, 0))
    return pl.pallas_call(
        functools.partial(_ffn_kernel, final_norm),
        grid=(m // tm,),
        in_specs=[pl.BlockSpec((tm, d), lambda i: (i, 0)), vec,
                  _resident((None, d, D_FF), lambda i: (layer, 0, 0)),
                  _resident((None, D_FF, d), lambda i: (layer, 0, 0)), vec],
        out_specs=pl.BlockSpec((tm, d), lambda i: (i, 0)),
        out_shape=jax.ShapeDtypeStruct((m, d), F32),
        name="ffn",
        compiler_params=_params(("parallel",), 48),
    )(x, g_ffn.reshape(1, d), w_up, w_down, g_final.reshape(1, d))


def _inproj_colscale():
    cs = jnp.ones((D_IN,), F32)
    cs = cs.at[OFF_RK:OFF_RK + RET_QK_W].set(RET_DK ** -0.5)
    cs = cs.at[OFF_DQ:OFF_DQ + DIFF_QK_W].set(DIFF_DK ** -0.5)
    cs = cs.at[OFF_CQ:OFF_CQ + CROSS_W].set(CROSS_DH ** -0.5)
    return cs


def kernel(x, mem, g_mix, w_in, g_ret, w_ret_o, lambda_q1, lambda_k1, lambda_q2, lambda_k2,
           g_diff, w_diff_o, g_mem, w_mem_kv, w_cross_o, w_out, g_ffn, w_up, w_down, g_final):
    batch, seq, d = x.shape
    xf = x.reshape(batch * seq, d)
    memf = mem.reshape(batch * mem.shape[1], d)
    colscale = _inproj_colscale()
    ones_kv = jnp.ones((2 * CROSS_W,), F32)
    w_in, w_mem_kv, w_ret_o, w_diff_o, w_cross_o, w_out, w_up, w_down = (
        w.astype(BF16) for w in (w_in, w_mem_kv, w_ret_o, w_diff_o, w_cross_o, w_out, w_up, w_down))
    for l in range(DEPTH):
        lam_init = 0.8 - 0.6 * math.exp(-0.3 * l)
        z = _inproj(xf, g_mix[l], w_in, l, colscale)
        ret = _retention(z, g_ret[l], batch, seq)
        da = _diff_attention(z, lambda_q1[l], lambda_k1[l], lambda_q2[l], lambda_k2[l],
                             g_diff[l], lam_init, batch, seq)
        kv = _inproj(memf, g_mem[l], w_mem_kv, l, ones_kv)
        ca = _cross_attention(z, kv, batch, seq)
        xf = _merge(xf, ret, da, ca, z, l, w_ret_o, w_diff_o, w_cross_o, w_out)
        xf = _ffn(xf, g_ffn[l], l, w_up, w_down, g_final, l == DEPTH - 1)
    return xf.reshape(batch, seq, d)
```

```python
import functools
import math

import jax
import jax.numpy as jnp
from jax import lax
from jax.experimental import pallas as pl
from jax.experimental.pallas import tpu as pltpu

F32 = jnp.float32
BF16 = jnp.bfloat16

D_MODEL = 1024
DEPTH = 2
N_MEM = 256
RET_HEADS = 4
RET_DK = D_MODEL // RET_HEADS
RET_DV = 2 * RET_DK
RET_KCHUNK = 256
DIFF_HEADS = 4
DIFF_DK = D_MODEL // (2 * DIFF_HEADS)
DIFF_DV = 2 * DIFF_DK
CROSS_HEADS = 4
CROSS_DH = D_MODEL // CROSS_HEADS
D_FF = 4 * D_MODEL
EPS = 1e-6
LOG2E = math.log2(math.e)

RET_QK_W = RET_HEADS * RET_DK
RET_V_W = RET_HEADS * RET_DV
DIFF_QK_W = DIFF_HEADS * 2 * DIFF_DK
DIFF_V_W = DIFF_HEADS * DIFF_DV
CROSS_W = CROSS_HEADS * CROSS_DH
GATE_W = 3 * D_MODEL
OFF_RQ = 0
OFF_RK = OFF_RQ + RET_QK_W
OFF_RV = OFF_RK + RET_QK_W
OFF_RG = OFF_RV + RET_V_W
OFF_DQ = OFF_RG + RET_V_W
OFF_DK = OFF_DQ + DIFF_QK_W
OFF_DV = OFF_DK + DIFF_QK_W
OFF_CQ = OFF_DV + DIFF_V_W
OFF_GATE = OFF_CQ + CROSS_W
D_IN = OFF_GATE + GATE_W

MIB = 1024 * 1024
LANES = 128
POS_SPLIT = 256

INPROJ_TM = 2048
INPROJ_TN = 1024
RET_TS = 1024
DIFF_TQ = 512
DIFF_TK = 512
CROSS_TQ = 1024
MERGE_TM = 512
FFN_TM = 512
FFN_TF = 1024


def _params(semantics, vmem_mib):
    return pltpu.CompilerParams(dimension_semantics=semantics,
                                vmem_limit_bytes=vmem_mib * MIB)


def _resident(shape, index_map):
    return pl.BlockSpec(shape, index_map, pipeline_mode=pl.Buffered(1))


def _rms(x, g):
    ms = jnp.mean(x * x, axis=-1, keepdims=True)
    return x * lax.rsqrt(ms + EPS) * g


def _sigmoid(x):
    return 1.0 / (1.0 + jnp.exp2(x * (-LOG2E)))


def _dot(a, b):
    return jnp.dot(a, b, preferred_element_type=F32)


def _dot_nt(a, b):
    return lax.dot_general(a, b, (((1,), (1,)), ((), ())), preferred_element_type=F32)


def _dot_tn(a, b):
    return lax.dot_general(a, b, (((0,), (0,)), ((), ())), preferred_element_type=F32)


def _inproj_kernel(x_ref, g_ref, w_ref, cs_ref, o_ref, h_ref):
    @pl.when(pl.program_id(1) == 0)
    def _():
        h_ref[...] = _rms(x_ref[...], g_ref[...]).astype(BF16)

    acc = _dot(h_ref[...], w_ref[...])
    o_ref[...] = (acc * cs_ref[...]).astype(o_ref.dtype)


def _inproj(x, g, w, layer, colscale):
    m, k = x.shape
    n = w.shape[2]
    tm = min(INPROJ_TM, m)
    tn = min(INPROJ_TN, n)
    return pl.pallas_call(
        _inproj_kernel,
        grid=(m // tm, n // tn),
        in_specs=[
            pl.BlockSpec((tm, k), lambda i, j: (i, 0)),
            pl.BlockSpec((1, k), lambda i, j: (0, 0)),
            pl.BlockSpec((None, k, tn), lambda i, j: (layer, 0, j)),
            pl.BlockSpec((1, tn), lambda i, j: (0, j)),
        ],
        out_specs=pl.BlockSpec((tm, tn), lambda i, j: (i, j)),
        out_shape=jax.ShapeDtypeStruct((m, n), BF16),
        scratch_shapes=[pltpu.VMEM((tm, k), BF16)],
        name="inproj",
        compiler_params=_params(("parallel", "arbitrary"), 56),
    )(x, g.reshape(1, k), w, colscale.reshape(1, n))


def _retention_kernel(chd_ref, q_ref, k_ref, v_ref, rg_ref, intra_ref, ind_ref, std_ref,
                      g_ref, o_ref, state_ref):
    h = pl.program_id(1)

    @pl.when(pl.program_id(2) == 0)
    def _():
        state_ref[...] = jnp.zeros_like(state_ref)

    ch_decay = chd_ref[h]
    gain = g_ref[...]
    c = RET_KCHUNK

    def lanes(table, width):
        return jnp.concatenate([table] * (width // LANES), axis=1)

    def chunk(ci, carry):
        rows = pl.ds(pl.multiple_of(ci * c, c), c)
        q = q_ref[rows, :]
        k = k_ref[rows, :]
        v = v_ref[rows, :]
        state = state_ref[...]
        scores = _dot_nt(q, k) * intra_ref[...]
        inner = _dot(scores.astype(BF16), v)
        qd = (q.astype(F32) * lanes(ind_ref[...], RET_DK)).astype(BF16)
        cross = _dot(qd, state.astype(BF16))
        kd = (k.astype(F32) * lanes(std_ref[...], RET_DK)).astype(BF16)
        state_ref[...] = ch_decay * state + _dot_tn(kd, v)
        o = inner + cross
        oc = o - jnp.mean(o, axis=-1, keepdims=True)
        y = _rms(oc, gain)
        rg = rg_ref[rows, :].astype(F32)
        o_ref[rows, :] = (rg * _sigmoid(rg) * y).astype(o_ref.dtype)
        return carry

    lax.fori_loop(0, q_ref.shape[0] // c, chunk, 0, unroll=True)


def _retention(z, g_ret, batch, seq):
    hh = RET_HEADS
    c = RET_KCHUNK
    ts = RET_TS
    nt = seq // ts
    log_g = jnp.log(1.0 - jnp.exp2(-5.0 - jnp.arange(hh, dtype=F32)))
    idx = jnp.arange(c, dtype=F32)
    rel = idx[:, None] - idx[None, :]
    intra = jnp.where(rel >= 0, jnp.exp(log_g[:, None, None] * jnp.maximum(rel, 0.0)), 0.0)
    in_decay = jnp.broadcast_to(jnp.exp(log_g[:, None] * (idx + 1.0))[:, :, None], (hh, c, LANES))
    st_decay = jnp.broadcast_to(jnp.exp(log_g[:, None] * (c - 1.0 - idx))[:, :, None],
                                (hh, c, LANES))
    ch_decay = jnp.exp(log_g * c)

    def row(b, h, t):
        return b * nt + t

    return pl.pallas_call(
        _retention_kernel,
        grid=(batch, hh, nt),
        in_specs=[
            pl.BlockSpec(memory_space=pltpu.SMEM),
            pl.BlockSpec((ts, RET_DK), lambda b, h, t: (row(b, h, t), OFF_RQ // RET_DK + h)),
            pl.BlockSpec((ts, RET_DK), lambda b, h, t: (row(b, h, t), OFF_RK // RET_DK + h)),
            pl.BlockSpec((ts, RET_DV), lambda b, h, t: (row(b, h, t), OFF_RV // RET_DV + h)),
            pl.BlockSpec((ts, RET_DV), lambda b, h, t: (row(b, h, t), OFF_RG // RET_DV + h)),
            pl.BlockSpec((None, c, c), lambda b, h, t: (h, 0, 0)),
            pl.BlockSpec((None, c, LANES), lambda b, h, t: (h, 0, 0)),
            pl.BlockSpec((None, c, LANES), lambda b, h, t: (h, 0, 0)),
            pl.BlockSpec((1, RET_DV), lambda b, h, t: (0, h)),
        ],
        out_specs=pl.BlockSpec((ts, RET_DV), lambda b, h, t: (row(b, h, t), h)),
        out_shape=jax.ShapeDtypeStruct((batch * seq, RET_V_W), BF16),
        scratch_shapes=[pltpu.VMEM((RET_DK, RET_DV), F32)],
        name="retention",
        compiler_params=_params(("parallel", "parallel", "arbitrary"), 40),
    )(ch_decay, z, z, z, z, intra, in_decay, st_decay, g_ret.reshape(1, RET_V_W))


def _diff_kernel(lam_init, slopes_ref, q_ref, k_ref, v_ref, lq1_ref, lk1_ref, lq2_ref, lk2_ref,
                 g_ref, o_ref, kaug_ref, qaug_ref, s_ref, mblk_ref, m_ref, l_ref, acc_ref):
    h = pl.program_id(1)
    seq = q_ref.shape[0]
    tq = DIFF_TQ
    tk = DIFF_TK
    d = DIFF_DK
    nq = seq // tq
    slope = slopes_ref[h]

    def build(j, carry):
        rows = pl.ds(pl.multiple_of(j * tk, tk), tk)
        pos = j * tk + lax.broadcasted_iota(jnp.int32, (tk, LANES), 0)
        lane = lax.broadcasted_iota(jnp.int32, (tk, LANES), 1)
        lo = pos & (POS_SPLIT - 1)
        kcols = jnp.where(lane == 0, (pos - lo).astype(F32) * slope,
                          jnp.where(lane == 1, lo.astype(F32) * slope,
                                    jnp.where(lane == 2, 1.0, 0.0))).astype(BF16)
        row = lax.broadcasted_iota(jnp.int32, (d, tq), 0)
        q0 = (j * tq).astype(F32)
        qrows = jnp.where(row < 2, 1.0, jnp.where(row == 2, -slope * q0, 0.0)).astype(BF16)
        for mp in range(2):
            kaug_ref[mp, rows, 0:d] = k_ref[rows, mp * d:(mp + 1) * d]
            kaug_ref[mp, rows, d:2 * d] = kcols
            qaug_ref[mp, j, 0:d] = q_ref[rows, mp * d:(mp + 1) * d].T
            qaug_ref[mp, j, d:2 * d] = qrows
        return carry

    lax.fori_loop(0, seq // tk, build, 0)

    lam = (jnp.exp(jnp.sum(lq1_ref[...] * lk1_ref[...], keepdims=True))
           - jnp.exp(jnp.sum(lq2_ref[...] * lk2_ref[...], keepdims=True)) + lam_init)

    def scores(qi, j, nblk, slot):
        krows = pl.ds(pl.multiple_of(j * tk, tk), nblk * tk)
        for mp in range(2):
            s = _dot(kaug_ref[mp, krows, :], qaug_ref[mp, qi])
            s_ref[slot, mp, 0:nblk * tk] = s
            mblk_ref[slot, mp] = jnp.max(s, axis=0, keepdims=True)

    def softmax_pv(j, nblk, slot, diag):
        nk = nblk * tk
        rows = pl.ds(pl.multiple_of(j * tk, tk), nk)
        v = v_ref[rows, :]
        for mp in range(2):
            s = s_ref[slot, mp, 0:nk]
            m_old = m_ref[mp]
            if diag:
                key = lax.broadcasted_iota(jnp.int32, (nk, tq), 0) - (nk - tk)
                qry = lax.broadcasted_iota(jnp.int32, (nk, tq), 1)
                s = jnp.where(key <= qry, s, -jnp.inf)
                m_blk = jnp.max(s, axis=0, keepdims=True)
            else:
                m_blk = mblk_ref[slot, mp]
            m_new = jnp.maximum(m_old, m_blk)
            alpha = jnp.exp(m_old - m_new)
            p = jnp.exp(s - m_new)
            l_ref[mp] = alpha * l_ref[mp] + jnp.sum(p, axis=0, keepdims=True)
            m_ref[mp] = m_new
            acc_ref[mp] = alpha * acc_ref[mp] + _dot_tn(v, p.astype(BF16))

    def step(qi, j, nblk, slot, diag, following):
        if following is not None:
            scores(qi, following[0], following[1], 1 - slot)
        softmax_pv(j, nblk, slot, diag)

    def tile(qi, carry):
        m_ref[...] = jnp.full_like(m_ref, -jnp.inf)
        l_ref[...] = jnp.zeros_like(l_ref)
        acc_ref[...] = jnp.zeros_like(acc_ref)

        def pair(t, c):
            step(qi, 4 * t, 2, 0, False, (4 * t + 2, 2))
            step(qi, 4 * t + 2, 2, 1, False, (4 * t + 4, 2))
            return c

        lax.fori_loop(0, qi // 4, pair, 0)
        rest = qi % 4

        @pl.when(rest == 0)
        def _():
            step(qi, qi, 1, 0, True, None)

        @pl.when(rest == 1)
        def _():
            step(qi, qi - 1, 2, 0, True, None)

        @pl.when(rest == 2)
        def _():
            step(qi, qi - 2, 2, 0, False, (qi, 1))
            step(qi, qi, 1, 1, True, None)

        @pl.when(rest == 3)
        def _():
            step(qi, qi - 3, 2, 0, False, (qi - 1, 2))
            step(qi, qi - 1, 2, 1, True, None)

        scores(jnp.minimum(qi + 1, nq - 1), 0, 2, 0)
        o = acc_ref[0] / l_ref[0] - lam * (acc_ref[1] / l_ref[1])
        o = o * lax.rsqrt(jnp.mean(o * o, axis=0, keepdims=True) + EPS)
        qrows = pl.ds(pl.multiple_of(qi * tq, tq), tq)
        o_ref[qrows, :] = (o.T * g_ref[...] * (1.0 - lam_init)).astype(o_ref.dtype)
        return carry

    scores(0, 0, 1, 0)
    lax.fori_loop(0, nq, tile, 0)


def _diff_attention(z, lq1, lk1, lq2, lk2, g_diff, lam_init, batch, seq):
    hh = DIFF_HEADS
    tq = DIFF_TQ
    assert DIFF_TQ == DIFF_TK and tq & (tq - 1) == 0 and seq % tq == 0
    assert 2 * tq <= seq <= POS_SPLIT * POS_SPLIT
    w = 2 * DIFF_DK
    slopes = [2.0 ** (-8.0 * (i + 1) / hh) for i in range(hh)]
    assert all(math.log2(s).is_integer() for s in slopes)
    vec = pl.BlockSpec((1, DIFF_DK), lambda b, h: (0, 0))
    return pl.pallas_call(
        functools.partial(_diff_kernel, lam_init),
        grid=(batch, hh),
        in_specs=[
            pl.BlockSpec(memory_space=pltpu.SMEM),
            pl.BlockSpec((seq, w), lambda b, h: (b, OFF_DQ // w + h)),
            pl.BlockSpec((seq, w), lambda b, h: (b, OFF_DK // w + h)),
            pl.BlockSpec((seq, DIFF_DV), lambda b, h: (b, OFF_DV // DIFF_DV + h)),
            vec, vec, vec, vec,
            pl.BlockSpec((1, DIFF_DV), lambda b, h: (0, h)),
        ],
        out_specs=pl.BlockSpec((seq, DIFF_DV), lambda b, h: (b, h)),
        out_shape=jax.ShapeDtypeStruct((batch * seq, DIFF_V_W), BF16),
        scratch_shapes=[pltpu.VMEM((2, seq, w), BF16), pltpu.VMEM((2, seq // tq, w, tq), BF16),
                        pltpu.VMEM((2, 2, 2 * DIFF_TK, tq), F32),
                        pltpu.VMEM((2, 2, 1, tq), F32),
                        pltpu.VMEM((2, 1, tq), F32), pltpu.VMEM((2, 1, tq), F32),
                        pltpu.VMEM((2, DIFF_DV, tq), F32)],
        name="diff_attention",
        compiler_params=_params(("parallel", "parallel"), 48),
    )(jnp.asarray(slopes, F32), z, z, z, lq1.reshape(1, -1), lk1.reshape(1, -1),
      lq2.reshape(1, -1), lk2.reshape(1, -1), g_diff.reshape(1, DIFF_V_W))


def _cross_kernel(q_ref, mk_ref, mv_ref, o_ref):
    dh = CROSS_DH
    for h in range(CROSS_HEADS):
        cols = slice(h * dh, (h + 1) * dh)
        s = _dot_nt(mk_ref[:, cols], q_ref[:, cols])
        p = jnp.exp(s - jnp.max(s, axis=0, keepdims=True))
        p = p * (1.0 / jnp.sum(p, axis=0, keepdims=True))
        o_ref[:, cols] = _dot_tn(p.astype(BF16), mv_ref[:, cols]).astype(o_ref.dtype)


def _cross_attention(z, kv, batch, seq):
    tq = CROSS_TQ
    nq = seq // tq
    return pl.pallas_call(
        _cross_kernel,
        grid=(batch, nq),
        in_specs=[
            pl.BlockSpec((tq, CROSS_W), lambda b, i: (b * nq + i, OFF_CQ // CROSS_W)),
            pl.BlockSpec((N_MEM, CROSS_W), lambda b, i: (b, 0)),
            pl.BlockSpec((N_MEM, CROSS_W), lambda b, i: (b, 1)),
        ],
        out_specs=pl.BlockSpec((tq, CROSS_W), lambda b, i: (b * nq + i, 0)),
        out_shape=jax.ShapeDtypeStruct((batch * seq, CROSS_W), BF16),
        name="cross_attention",
        compiler_params=_params(("parallel", "arbitrary"), 40),
    )(z, kv, kv)


def _merge_kernel(x_ref, ret_ref, da_ref, ca_ref, g0_ref, g1_ref, g2_ref,
                  wr_ref, wd_ref, wc_ref, wo_ref, o_ref):
    merged = _sigmoid(g0_ref[...].astype(F32)) * _dot(ret_ref[...], wr_ref[...])
    merged += _sigmoid(g1_ref[...].astype(F32)) * _dot(da_ref[...], wd_ref[...])
    merged += _sigmoid(g2_ref[...].astype(F32)) * _dot(ca_ref[...], wc_ref[...])
    o_ref[...] = x_ref[...] + _dot(merged.astype(BF16), wo_ref[...])


def _merge(x, ret, da, ca, z, layer, w_ret_o, w_diff_o, w_cross_o, w_out):
    m, d = x.shape
    tm = MERGE_TM
    gate0 = OFF_GATE // d

    def rows(width):
        return pl.BlockSpec((tm, width), lambda i: (i, 0))

    def gate(n):
        return pl.BlockSpec((tm, d), lambda i: (i, gate0 + n))

    def weight(k):
        return _resident((None, k, d), lambda i: (layer, 0, 0))

    return pl.pallas_call(
        _merge_kernel,
        grid=(m // tm,),
        in_specs=[rows(d), rows(RET_V_W), rows(DIFF_V_W), rows(CROSS_W),
                  gate(0), gate(1), gate(2),
                  weight(RET_V_W), weight(DIFF_V_W), weight(CROSS_W), weight(d)],
        out_specs=rows(d),
        out_shape=jax.ShapeDtypeStruct((m, d), F32),
        name="merge",
        compiler_params=_params(("parallel",), 48),
    )(x, ret, da, ca, z, z, z, w_ret_o, w_diff_o, w_cross_o, w_out)


def _ffn_kernel(final_norm, x_ref, g_ref, wu_ref, wd_ref, gf_ref, o_ref):
    x = x_ref[...]
    h = _rms(x, g_ref[...]).astype(BF16)
    acc = x
    for f in range(D_FF // FFN_TF):
        cols = slice(f * FFN_TF, (f + 1) * FFN_TF)
        u = jnp.maximum(_dot(h, wu_ref[:, cols]), 0.0)
        acc = acc + _dot((u * u).astype(BF16), wd_ref[cols, :])
    if final_norm:
        acc = _rms(acc, gf_ref[...])
    o_ref[...] = acc


def _ffn(x, g_ffn, layer, w_up, w_down, g_final, final_norm):
    m, d = x.shape
    tm = FFN_TM
    vec = pl.BlockSpec((1, d), lambda i: (0, 0))
    return pl.pallas_call(
        functools.partial(_ffn_kernel, final_norm),
        grid=(m // tm,),
        in_specs=[pl.BlockSpec((tm, d), lambda i: (i, 0)), vec,
                  _resident((None, d, D_FF), lambda i: (layer, 0, 0)),
                  _resident((None, D_FF, d), lambda i: (layer, 0, 0)), vec],
        out_specs=pl.BlockSpec((tm, d), lambda i: (i, 0)),
        out_shape=jax.ShapeDtypeStruct((m, d), F32),
        name="ffn",
        compiler_params=_params(("parallel",), 48),
    )(x, g_ffn.reshape(1, d), w_up, w_down, g_final.reshape(1, d))


def _inproj_colscale():
    cs = jnp.ones((D_IN,), F32)
    cs = cs.at[OFF_RK:OFF_RK + RET_QK_W].set(RET_DK ** -0.5)
    cs = cs.at[OFF_DQ:OFF_DQ + DIFF_QK_W].set(DIFF_DK ** -0.5)
    cs = cs.at[OFF_CQ:OFF_CQ + CROSS_W].set(CROSS_DH ** -0.5)
    return cs


def kernel(x, mem, g_mix, w_in, g_ret, w_ret_o, lambda_q1, lambda_k1, lambda_q2, lambda_k2,
           g_diff, w_diff_o, g_mem, w_mem_kv, w_cross_o, w_out, g_ffn, w_up, w_down, g_final):
    batch, seq, d = x.shape
    xf = x.reshape(batch * seq, d)
    memf = mem.reshape(batch * mem.shape[1], d)
    colscale = _inproj_colscale()
    ones_kv = jnp.ones((2 * CROSS_W,), F32)
    w_in, w_mem_kv, w_ret_o, w_diff_o, w_cross_o, w_out, w_up, w_down = (
        w.astype(BF16) for w in (w_in, w_mem_kv, w_ret_o, w_diff_o, w_cross_o, w_out, w_up, w_down))
    for l in range(DEPTH):
        lam_init = 0.8 - 0.6 * math.exp(-0.3 * l)
        z = _inproj(xf, g_mix[l], w_in, l, colscale)
        ret = _retention(z, g_ret[l], batch, seq)
        da = _diff_attention(z, lambda_q1[l], lambda_k1[l], lambda_q2[l], lambda_k2[l],
                             g_diff[l], lam_init, batch, seq)
        kv = _inproj(memf, g_mem[l], w_mem_kv, l, ones_kv)
        ca = _cross_attention(z, kv, batch, seq)
        xf = _merge(xf, ret, da, ca, z, l, w_ret_o, w_diff_o, w_cross_o, w_out)
        xf = _ffn(xf, g_ffn[l], l, w_up, w_down, g_final, l == DEPTH - 1)
    return xf.reshape(batch, seq, d)
```

```python
import functools
import math

import jax
import jax.numpy as jnp
from jax import lax
from jax.experimental import pallas as pl
from jax.experimental.pallas import tpu as pltpu

F32 = jnp.float32
BF16 = jnp.bfloat16

D_MODEL = 1024
DEPTH = 2
N_MEM = 256
RET_HEADS = 4
RET_DK = D_MODEL // RET_HEADS
RET_DV = 2 * RET_DK
RET_KCHUNK = 256
DIFF_HEADS = 4
DIFF_DK = D_MODEL // (2 * DIFF_HEADS)
DIFF_DV = 2 * DIFF_DK
CROSS_HEADS = 4
CROSS_DH = D_MODEL // CROSS_HEADS
D_FF = 4 * D_MODEL
EPS = 1e-6
LOG2E = math.log2(math.e)

RET_QK_W = RET_HEADS * RET_DK
RET_V_W = RET_HEADS * RET_DV
DIFF_QK_W = DIFF_HEADS * 2 * DIFF_DK
DIFF_V_W = DIFF_HEADS * DIFF_DV
CROSS_W = CROSS_HEADS * CROSS_DH
GATE_W = 3 * D_MODEL
OFF_RQ = 0
OFF_RK = OFF_RQ + RET_QK_W
OFF_RV = OFF_RK + RET_QK_W
OFF_RG = OFF_RV + RET_V_W
OFF_DQ = OFF_RG + RET_V_W
OFF_DK = OFF_DQ + DIFF_QK_W
OFF_DV = OFF_DK + DIFF_QK_W
OFF_CQ = OFF_DV + DIFF_V_W
OFF_GATE = OFF_CQ + CROSS_W
D_IN = OFF_GATE + GATE_W

MIB = 1024 * 1024
LANES = 128
POS_SPLIT = 256

INPROJ_TM = 2048
INPROJ_TN = 1024
RET_TS = 1024
DIFF_TQ = 512
DIFF_TK = 512
CROSS_TQ = 1024
MERGE_TM = 512
FFN_TM = 1024
FFN_TF = 1024


def _params(semantics, vmem_mib):
    return pltpu.CompilerParams(dimension_semantics=semantics,
                                vmem_limit_bytes=vmem_mib * MIB)


def _resident(shape, index_map):
    return pl.BlockSpec(shape, index_map, pipeline_mode=pl.Buffered(1))


def _rms(x, g):
    ms = jnp.mean(x * x, axis=-1, keepdims=True)
    return x * lax.rsqrt(ms + EPS) * g


def _sigmoid(x):
    return 1.0 / (1.0 + jnp.exp2(x * (-LOG2E)))


def _dot(a, b):
    return jnp.dot(a, b, preferred_element_type=F32)


def _dot_nt(a, b):
    return lax.dot_general(a, b, (((1,), (1,)), ((), ())), preferred_element_type=F32)


def _dot_tn(a, b):
    return lax.dot_general(a, b, (((0,), (0,)), ((), ())), preferred_element_type=F32)


def _inproj_kernel(x_ref, g_ref, w_ref, cs_ref, o_ref, h_ref):
    @pl.when(pl.program_id(1) == 0)
    def _():
        h_ref[...] = _rms(x_ref[...], g_ref[...]).astype(BF16)

    acc = _dot(h_ref[...], w_ref[...])
    o_ref[...] = (acc * cs_ref[...]).astype(o_ref.dtype)


def _inproj(x, g, w, layer, colscale):
    m, k = x.shape
    n = w.shape[2]
    tm = min(INPROJ_TM, m)
    tn = min(INPROJ_TN, n)
    return pl.pallas_call(
        _inproj_kernel,
        grid=(m // tm, n // tn),
        in_specs=[
            pl.BlockSpec((tm, k), lambda i, j: (i, 0)),
            pl.BlockSpec((1, k), lambda i, j: (0, 0)),
            pl.BlockSpec((None, k, tn), lambda i, j: (layer, 0, j)),
            pl.BlockSpec((1, tn), lambda i, j: (0, j)),
        ],
        out_specs=pl.BlockSpec((tm, tn), lambda i, j: (i, j)),
        out_shape=jax.ShapeDtypeStruct((m, n), BF16),
        scratch_shapes=[pltpu.VMEM((tm, k), BF16)],
        name="inproj",
        compiler_params=_params(("parallel", "arbitrary"), 56),
    )(x, g.reshape(1, k), w, colscale.reshape(1, n))


def _retention_kernel(chd_ref, q_ref, k_ref, v_ref, intra_ref, ind_ref, std_ref, o_ref, state_ref):
    h = pl.program_id(1)

    @pl.when(pl.program_id(2) == 0)
    def _():
        state_ref[...] = jnp.zeros_like(state_ref)

    ch_decay = chd_ref[h]
    c = RET_KCHUNK

    def lanes(table, width):
        return jnp.concatenate([table] * (width // LANES), axis=1)

    def chunk(ci, carry):
        rows = pl.ds(pl.multiple_of(ci * c, c), c)
        q = q_ref[rows, :]
        k = k_ref[rows, :]
        v = v_ref[rows, :]
        state = state_ref[...]
        scores = _dot_nt(q, k) * intra_ref[...]
        inner = _dot(scores.astype(BF16), v)
        qd = (q.astype(F32) * lanes(ind_ref[...], RET_DK)).astype(BF16)
        cross = _dot(qd, state.astype(BF16))
        kd = (k.astype(F32) * lanes(std_ref[...], RET_DK)).astype(BF16)
        state_ref[...] = ch_decay * state + _dot_tn(kd, v)
        o_ref[rows, :] = (inner + cross).astype(o_ref.dtype)
        return carry

    lax.fori_loop(0, q_ref.shape[0] // c, chunk, 0, unroll=True)


def _retention(z, batch, seq):
    hh = RET_HEADS
    c = RET_KCHUNK
    ts = RET_TS
    nt = seq // ts
    log_g = jnp.log(1.0 - jnp.exp2(-5.0 - jnp.arange(hh, dtype=F32)))
    idx = jnp.arange(c, dtype=F32)
    rel = idx[:, None] - idx[None, :]
    intra = jnp.where(rel >= 0, jnp.exp(log_g[:, None, None] * jnp.maximum(rel, 0.0)), 0.0)
    in_decay = jnp.broadcast_to(jnp.exp(log_g[:, None] * (idx + 1.0))[:, :, None], (hh, c, LANES))
    st_decay = jnp.broadcast_to(jnp.exp(log_g[:, None] * (c - 1.0 - idx))[:, :, None],
                                (hh, c, LANES))
    ch_decay = jnp.exp(log_g * c)

    def row(b, h, t):
        return b * nt + t

    return pl.pallas_call(
        _retention_kernel,
        grid=(batch, hh, nt),
        in_specs=[
            pl.BlockSpec(memory_space=pltpu.SMEM),
            pl.BlockSpec((ts, RET_DK), lambda b, h, t: (row(b, h, t), OFF_RQ // RET_DK + h)),
            pl.BlockSpec((ts, RET_DK), lambda b, h, t: (row(b, h, t), OFF_RK // RET_DK + h)),
            pl.BlockSpec((ts, RET_DV), lambda b, h, t: (row(b, h, t), OFF_RV // RET_DV + h)),
            pl.BlockSpec((None, c, c), lambda b, h, t: (h, 0, 0)),
            pl.BlockSpec((None, c, LANES), lambda b, h, t: (h, 0, 0)),
            pl.BlockSpec((None, c, LANES), lambda b, h, t: (h, 0, 0)),
        ],
        out_specs=pl.BlockSpec((ts, RET_DV), lambda b, h, t: (row(b, h, t), h)),
        out_shape=jax.ShapeDtypeStruct((batch * seq, RET_V_W), BF16),
        scratch_shapes=[pltpu.VMEM((RET_DK, RET_DV), F32)],
        name="retention",
        compiler_params=_params(("parallel", "parallel", "arbitrary"), 40),
    )(ch_decay, z, z, z, intra, in_decay, st_decay)


def _diff_kernel(lam_init, slopes_ref, q_ref, k_ref, v_ref, lq1_ref, lk1_ref, lq2_ref, lk2_ref,
                 g_ref, o_ref, kaug_ref, qaug_ref, s_ref, mblk_ref, m_ref, l_ref, acc_ref):
    h = pl.program_id(1)
    seq = q_ref.shape[0]
    tq = DIFF_TQ
    tk = DIFF_TK
    d = DIFF_DK
    nq = seq // tq
    slope = slopes_ref[h]

    def build(j, carry):
        rows = pl.ds(pl.multiple_of(j * tk, tk), tk)
        pos = j * tk + lax.broadcasted_iota(jnp.int32, (tk, LANES), 0)
        lane = lax.broadcasted_iota(jnp.int32, (tk, LANES), 1)
        lo = pos & (POS_SPLIT - 1)
        kcols = jnp.where(lane == 0, (pos - lo).astype(F32) * slope,
                          jnp.where(lane == 1, lo.astype(F32) * slope,
                                    jnp.where(lane == 2, 1.0, 0.0))).astype(BF16)
        row = lax.broadcasted_iota(jnp.int32, (d, tq), 0)
        q0 = (j * tq).astype(F32)
        qrows = jnp.where(row < 2, 1.0, jnp.where(row == 2, -slope * q0, 0.0)).astype(BF16)
        for mp in range(2):
            kaug_ref[mp, rows, 0:d] = k_ref[rows, mp * d:(mp + 1) * d]
            kaug_ref[mp, rows, d:2 * d] = kcols
            qaug_ref[mp, j, 0:d] = q_ref[rows, mp * d:(mp + 1) * d].T
            qaug_ref[mp, j, d:2 * d] = qrows
        return carry

    lax.fori_loop(0, seq // tk, build, 0)

    lam = (jnp.exp(jnp.sum(lq1_ref[...] * lk1_ref[...], keepdims=True))
           - jnp.exp(jnp.sum(lq2_ref[...] * lk2_ref[...], keepdims=True)) + lam_init)

    def scores(qi, j, nblk, slot):
        krows = pl.ds(pl.multiple_of(j * tk, tk), nblk * tk)
        for mp in range(2):
            s = _dot(kaug_ref[mp, krows, :], qaug_ref[mp, qi])
            s_ref[slot, mp, 0:nblk * tk] = s
            mblk_ref[slot, mp] = jnp.max(s, axis=0, keepdims=True)

    def softmax_pv(j, nblk, slot, diag):
        nk = nblk * tk
        rows = pl.ds(pl.multiple_of(j * tk, tk), nk)
        v = v_ref[rows, :]
        for mp in range(2):
            s = s_ref[slot, mp, 0:nk]
            m_old = m_ref[mp]
            if diag:
                key = lax.broadcasted_iota(jnp.int32, (nk, tq), 0) - (nk - tk)
                qry = lax.broadcasted_iota(jnp.int32, (nk, tq), 1)
                s = jnp.where(key <= qry, s, -jnp.inf)
                m_blk = jnp.max(s, axis=0, keepdims=True)
            else:
                m_blk = mblk_ref[slot, mp]
            m_new = jnp.maximum(m_old, m_blk)
            alpha = jnp.exp(m_old - m_new)
            p = jnp.exp(s - m_new)
            l_ref[mp] = alpha * l_ref[mp] + jnp.sum(p, axis=0, keepdims=True)
            m_ref[mp] = m_new
            acc_ref[mp] = alpha * acc_ref[mp] + _dot_tn(v, p.astype(BF16))

    def step(qi, j, nblk, slot, diag, following):
        if following is not None:
            scores(qi, following[0], following[1], 1 - slot)
        softmax_pv(j, nblk, slot, diag)

    def tile(qi, carry):
        m_ref[...] = jnp.full_like(m_ref, -jnp.inf)
        l_ref[...] = jnp.zeros_like(l_ref)
        acc_ref[...] = jnp.zeros_like(acc_ref)

        def pair(t, c):
            step(qi, 4 * t, 2, 0, False, (4 * t + 2, 2))
            step(qi, 4 * t + 2, 2, 1, False, (4 * t + 4, 2))
            return c

        lax.fori_loop(0, qi // 4, pair, 0)
        rest = qi % 4

        @pl.when(rest == 0)
        def _():
            step(qi, qi, 1, 0, True, None)

        @pl.when(rest == 1)
        def _():
            step(qi, qi - 1, 2, 0, True, None)

        @pl.when(rest == 2)
        def _():
            step(qi, qi - 2, 2, 0, False, (qi, 1))
            step(qi, qi, 1, 1, True, None)

        @pl.when(rest == 3)
        def _():
            step(qi, qi - 3, 2, 0, False, (qi - 1, 2))
            step(qi, qi - 1, 2, 1, True, None)

        scores(jnp.minimum(qi + 1, nq - 1), 0, 2, 0)
        o = acc_ref[0] / l_ref[0] - lam * (acc_ref[1] / l_ref[1])
        o = o * lax.rsqrt(jnp.mean(o * o, axis=0, keepdims=True) + EPS)
        qrows = pl.ds(pl.multiple_of(qi * tq, tq), tq)
        o_ref[qrows, :] = (o.T * g_ref[...] * (1.0 - lam_init)).astype(o_ref.dtype)
        return carry

    scores(0, 0, 1, 0)
    lax.fori_loop(0, nq, tile, 0)


def _diff_attention(z, lq1, lk1, lq2, lk2, g_diff, lam_init, batch, seq):
    hh = DIFF_HEADS
    tq = DIFF_TQ
    assert DIFF_TQ == DIFF_TK and tq & (tq - 1) == 0 and seq % tq == 0
    assert 2 * tq <= seq <= POS_SPLIT * POS_SPLIT
    w = 2 * DIFF_DK
    slopes = [2.0 ** (-8.0 * (i + 1) / hh) for i in range(hh)]
    assert all(math.log2(s).is_integer() for s in slopes)
    vec = pl.BlockSpec((1, DIFF_DK), lambda b, h: (0, 0))
    return pl.pallas_call(
        functools.partial(_diff_kernel, lam_init),
        grid=(batch, hh),
        in_specs=[
            pl.BlockSpec(memory_space=pltpu.SMEM),
            pl.BlockSpec((seq, w), lambda b, h: (b, OFF_DQ // w + h)),
            pl.BlockSpec((seq, w), lambda b, h: (b, OFF_DK // w + h)),
            pl.BlockSpec((seq, DIFF_DV), lambda b, h: (b, OFF_DV // DIFF_DV + h)),
            vec, vec, vec, vec,
            pl.BlockSpec((1, DIFF_DV), lambda b, h: (0, h)),
        ],
        out_specs=pl.BlockSpec((seq, DIFF_DV), lambda b, h: (b, h)),
        out_shape=jax.ShapeDtypeStruct((batch * seq, DIFF_V_W), BF16),
        scratch_shapes=[pltpu.VMEM((2, seq, w), BF16), pltpu.VMEM((2, seq // tq, w, tq), BF16),
                        pltpu.VMEM((2, 2, 2 * DIFF_TK, tq), F32),
                        pltpu.VMEM((2, 2, 1, tq), F32),
                        pltpu.VMEM((2, 1, tq), F32), pltpu.VMEM((2, 1, tq), F32),
                        pltpu.VMEM((2, DIFF_DV, tq), F32)],
        name="diff_attention",
        compiler_params=_params(("parallel", "parallel"), 48),
    )(jnp.asarray(slopes, F32), z, z, z, lq1.reshape(1, -1), lk1.reshape(1, -1),
      lq2.reshape(1, -1), lk2.reshape(1, -1), g_diff.reshape(1, DIFF_V_W))


def _cross_kernel(q_ref, mk_ref, mv_ref, o_ref):
    dh = CROSS_DH
    for h in range(CROSS_HEADS):
        cols = slice(h * dh, (h + 1) * dh)
        s = _dot_nt(mk_ref[:, cols], q_ref[:, cols])
        p = jnp.exp(s - jnp.max(s, axis=0, keepdims=True))
        p = p * (1.0 / jnp.sum(p, axis=0, keepdims=True))
        o_ref[:, cols] = _dot_tn(p.astype(BF16), mv_ref[:, cols]).astype(o_ref.dtype)


def _cross_attention(z, kv, batch, seq):
    tq = CROSS_TQ
    nq = seq // tq
    return pl.pallas_call(
        _cross_kernel,
        grid=(batch, nq),
        in_specs=[
            pl.BlockSpec((tq, CROSS_W), lambda b, i: (b * nq + i, OFF_CQ // CROSS_W)),
            pl.BlockSpec((N_MEM, CROSS_W), lambda b, i: (b, 0)),
            pl.BlockSpec((N_MEM, CROSS_W), lambda b, i: (b, 1)),
        ],
        out_specs=pl.BlockSpec((tq, CROSS_W), lambda b, i: (b * nq + i, 0)),
        out_shape=jax.ShapeDtypeStruct((batch * seq, CROSS_W), BF16),
        name="cross_attention",
        compiler_params=_params(("parallel", "arbitrary"), 40),
    )(z, kv, kv)


def _merge_kernel(x_ref, ret_ref, rg_ref, da_ref, ca_ref, g0_ref, g1_ref, g2_ref,
                  gr_ref, wr_ref, wd_ref, wc_ref, wo_ref, o_ref):
    merged = _sigmoid(g1_ref[...].astype(F32)) * _dot(da_ref[...], wd_ref[...])
    merged += _sigmoid(g2_ref[...].astype(F32)) * _dot(ca_ref[...], wc_ref[...])
    y_ret = None
    for h in range(RET_HEADS):
        cols = slice(h * RET_DV, (h + 1) * RET_DV)
        o = ret_ref[:, cols].astype(F32)
        oc = o - jnp.mean(o, axis=-1, keepdims=True)
        y = _rms(oc, gr_ref[:, cols])
        rg = rg_ref[:, cols].astype(F32)
        part = _dot((rg * _sigmoid(rg) * y).astype(BF16), wr_ref[cols, :])
        y_ret = part if y_ret is None else y_ret + part
    merged += _sigmoid(g0_ref[...].astype(F32)) * y_ret
    o_ref[...] = x_ref[...] + _dot(merged.astype(BF16), wo_ref[...])


def _merge(x, ret, da, ca, z, g_ret, layer, w_ret_o, w_diff_o, w_cross_o, w_out):
    m, d = x.shape
    tm = MERGE_TM
    gate0 = OFF_GATE // d

    def rows(width):
        return pl.BlockSpec((tm, width), lambda i: (i, 0))

    def gate(n):
        return pl.BlockSpec((tm, d), lambda i: (i, gate0 + n))

    def weight(k):
        return _resident((None, k, d), lambda i: (layer, 0, 0))

    return pl.pallas_call(
        _merge_kernel,
        grid=(m // tm,),
        in_specs=[rows(d), rows(RET_V_W),
                  pl.BlockSpec((tm, RET_V_W), lambda i: (i, OFF_RG // RET_V_W)),
                  rows(DIFF_V_W), rows(CROSS_W), gate(0), gate(1), gate(2),
                  pl.BlockSpec((1, RET_V_W), lambda i: (0, 0)),
                  weight(RET_V_W), weight(DIFF_V_W), weight(CROSS_W), weight(d)],
        out_specs=rows(d),
        out_shape=jax.ShapeDtypeStruct((m, d), F32),
        name="merge",
        compiler_params=_params(("parallel",), 56),
    )(x, ret, z, da, ca, z, z, z, g_ret.reshape(1, RET_V_W), w_ret_o, w_diff_o, w_cross_o, w_out)


def _ffn_kernel(final_norm, x_ref, g_ref, wu_ref, wd_ref, gf_ref, o_ref):
    x = x_ref[...]
    h = _rms(x, g_ref[...]).astype(BF16)
    acc = x
    for f in range(D_FF // FFN_TF):
        cols = slice(f * FFN_TF, (f + 1) * FFN_TF)
        u = jnp.maximum(_dot(h, wu_ref[:, cols]), 0.0)
        acc = acc + _dot((u * u).astype(BF16), wd_ref[cols, :])
    if final_norm:
        acc = _rms(acc, gf_ref[...])
    o_ref[...] = acc


def _ffn(x, g_ffn, layer, w_up, w_down, g_final, final_norm):
    m, d = x.shape
    tm = FFN_TM
    vec = pl.BlockSpec((1, d), lambda i: (0, 0))
    return pl.pallas_call(
        functools.partial(_ffn_kernel, final_norm),
        grid=(m // tm,),
        in_specs=[pl.BlockSpec((tm, d), lambda i: (i, 0)), vec,
                  _resident((None, d, D_FF), lambda i: (layer, 0, 0)),
                  _resident((None, D_FF, d), lambda i: (layer, 0, 0)), vec],
        out_specs=pl.BlockSpec((tm, d), lambda i: (i, 0)),
        out_shape=jax.ShapeDtypeStruct((m, d), F32),
        name="ffn",
        compiler_params=_params(("parallel",), 56),
    )(x, g_ffn.reshape(1, d), w_up, w_down, g_final.reshape(1, d))


def _inproj_colscale():
    cs = jnp.ones((D_IN,), F32)
    cs = cs.at[OFF_RK:OFF_RK + RET_QK_W].set(RET_DK ** -0.5)
    cs = cs.at[OFF_DQ:OFF_DQ + DIFF_QK_W].set(DIFF_DK ** -0.5)
    cs = cs.at[OFF_CQ:OFF_CQ + CROSS_W].set(CROSS_DH ** -0.5)
    return cs


def kernel(x, mem, g_mix, w_in, g_ret, w_ret_o, lambda_q1, lambda_k1, lambda_q2, lambda_k2,
           g_diff, w_diff_o, g_mem, w_mem_kv, w_cross_o, w_out, g_ffn, w_up, w_down, g_final):
    batch, seq, d = x.shape
    xf = x.reshape(batch * seq, d)
    memf = mem.reshape(batch * mem.shape[1], d)
    colscale = _inproj_colscale()
    ones_kv = jnp.ones((2 * CROSS_W,), F32)
    w_in, w_mem_kv, w_ret_o, w_diff_o, w_cross_o, w_out, w_up, w_down = (
        w.astype(BF16) for w in (w_in, w_mem_kv, w_ret_o, w_diff_o, w_cross_o, w_out, w_up, w_down))
    for l in range(DEPTH):
        lam_init = 0.8 - 0.6 * math.exp(-0.3 * l)
        z = _inproj(xf, g_mix[l], w_in, l, colscale)
        ret = _retention(z, batch, seq)
        da = _diff_attention(z, lambda_q1[l], lambda_k1[l], lambda_q2[l], lambda_k2[l],
                             g_diff[l], lam_init, batch, seq)
        kv = _inproj(memf, g_mem[l], w_mem_kv, l, ones_kv)
        ca = _cross_attention(z, kv, batch, seq)
        xf = _merge(xf, ret, da, ca, z, g_ret[l], l, w_ret_o, w_diff_o, w_cross_o, w_out)
        xf = _ffn(xf, g_ffn[l], l, w_up, w_down, g_final, l == DEPTH - 1)
    return xf.reshape(batch, seq, d)
```

```python
import functools
import math

import jax
import jax.numpy as jnp
from jax import lax
from jax.experimental import pallas as pl
from jax.experimental.pallas import tpu as pltpu

F32 = jnp.float32
BF16 = jnp.bfloat16

D_MODEL = 1024
DEPTH = 2
N_MEM = 256
RET_HEADS = 4
RET_DK = D_MODEL // RET_HEADS
RET_DV = 2 * RET_DK
RET_KCHUNK = 256
DIFF_HEADS = 4
DIFF_DK = D_MODEL // (2 * DIFF_HEADS)
DIFF_DV = 2 * DIFF_DK
CROSS_HEADS = 4
CROSS_DH = D_MODEL // CROSS_HEADS
D_FF = 4 * D_MODEL
EPS = 1e-6
LOG2E = math.log2(math.e)

RET_QK_W = RET_HEADS * RET_DK
RET_V_W = RET_HEADS * RET_DV
DIFF_QK_W = DIFF_HEADS * 2 * DIFF_DK
DIFF_V_W = DIFF_HEADS * DIFF_DV
CROSS_W = CROSS_HEADS * CROSS_DH
GATE_W = 3 * D_MODEL
OFF_RQ = 0
OFF_RK = OFF_RQ + RET_QK_W
OFF_RV = OFF_RK + RET_QK_W
OFF_RG = OFF_RV + RET_V_W
OFF_DQ = OFF_RG + RET_V_W
OFF_DK = OFF_DQ + DIFF_QK_W
OFF_DV = OFF_DK + DIFF_QK_W
OFF_CQ = OFF_DV + DIFF_V_W
OFF_GATE = OFF_CQ + CROSS_W
D_IN = OFF_GATE + GATE_W

MIB = 1024 * 1024
LANES = 128
POS_SPLIT = 256

INPROJ_TM = 1024
INPROJ_TN = 3328
INPROJ_TC = 256
RET_UNROLL = 4
DIFF_TQ = 512
DIFF_TK = 512
CROSS_TQ = 1024
MERGE_TM = 512
FFN_TM = 1024
FFN_TF = 1024


def _params(semantics, vmem_mib):
    return pltpu.CompilerParams(dimension_semantics=semantics,
                                vmem_limit_bytes=vmem_mib * MIB)


def _resident(shape, index_map):
    return pl.BlockSpec(shape, index_map, pipeline_mode=pl.Buffered(1))


def _rms(x, g):
    ms = jnp.mean(x * x, axis=-1, keepdims=True)
    return x * lax.rsqrt(ms + EPS) * g


def _sigmoid(x):
    return 1.0 / (1.0 + jnp.exp2(x * (-LOG2E)))


def _dot(a, b):
    return jnp.dot(a, b, preferred_element_type=F32)


def _dot_nt(a, b):
    return lax.dot_general(a, b, (((1,), (1,)), ((), ())), preferred_element_type=F32)


def _dot_tn(a, b):
    return lax.dot_general(a, b, (((0,), (0,)), ((), ())), preferred_element_type=F32)


def _inproj_kernel(x_ref, g_ref, w_ref, cs_ref, o_ref, h_ref):
    @pl.when(pl.program_id(1) == 0)
    def _():
        h_ref[...] = _rms(x_ref[...], g_ref[...]).astype(BF16)

    for c0 in range(0, o_ref.shape[1], INPROJ_TC):
        cols = slice(c0, c0 + INPROJ_TC)
        acc = _dot(h_ref[...], w_ref[:, cols])
        o_ref[:, cols] = (acc * cs_ref[:, cols]).astype(o_ref.dtype)


def _inproj(x, g, w, layer, colscale):
    m, k = x.shape
    n = w.shape[2]
    tm = min(INPROJ_TM, m)
    tn = min(INPROJ_TN, n)
    return pl.pallas_call(
        _inproj_kernel,
        grid=(m // tm, n // tn),
        in_specs=[
            pl.BlockSpec((tm, k), lambda i, j: (i, 0)),
            pl.BlockSpec((1, k), lambda i, j: (0, 0)),
            pl.BlockSpec((None, k, tn), lambda i, j: (layer, 0, j)),
            pl.BlockSpec((1, tn), lambda i, j: (0, j)),
        ],
        out_specs=pl.BlockSpec((tm, tn), lambda i, j: (i, j)),
        out_shape=jax.ShapeDtypeStruct((m, n), BF16),
        scratch_shapes=[pltpu.VMEM((tm, k), BF16)],
        name="inproj",
        compiler_params=_params(("parallel", "arbitrary"), 56),
    )(x, g.reshape(1, k), w, colscale.reshape(1, n))


def _retention_kernel(chd_ref, q_ref, k_ref, v_ref, intra_ref, ind_ref, std_ref, o_ref, state_ref):
    h = pl.program_id(1)
    state_ref[...] = jnp.zeros_like(state_ref)
    ch_decay = chd_ref[h]
    c = RET_KCHUNK

    def lanes(table, width):
        return jnp.concatenate([table] * (width // LANES), axis=1)

    def chunk(ci):
        rows = pl.ds(pl.multiple_of(ci * c, c), c)
        q = q_ref[rows, :]
        k = k_ref[rows, :]
        v = v_ref[rows, :]
        state = state_ref[...]
        scores = _dot_nt(q, k) * intra_ref[...]
        inner = _dot(scores.astype(BF16), v)
        qd = (q.astype(F32) * lanes(ind_ref[...], RET_DK)).astype(BF16)
        cross = _dot(qd, state.astype(BF16))
        kd = (k.astype(F32) * lanes(std_ref[...], RET_DK)).astype(BF16)
        state_ref[...] = ch_decay * state + _dot_tn(kd, v)
        o_ref[rows, :] = (inner + cross).astype(o_ref.dtype)

    def group(gi, carry):
        for ci in range(RET_UNROLL):
            chunk(gi * RET_UNROLL + ci)
        return carry

    lax.fori_loop(0, q_ref.shape[0] // (c * RET_UNROLL), group, 0)


def _retention(z, batch, seq):
    hh = RET_HEADS
    c = RET_KCHUNK
    assert seq % (c * RET_UNROLL) == 0
    log_g = jnp.log(1.0 - jnp.exp2(-5.0 - jnp.arange(hh, dtype=F32)))
    idx = jnp.arange(c, dtype=F32)
    rel = idx[:, None] - idx[None, :]
    intra = jnp.where(rel >= 0, jnp.exp(log_g[:, None, None] * jnp.maximum(rel, 0.0)), 0.0)
    in_decay = jnp.broadcast_to(jnp.exp(log_g[:, None] * (idx + 1.0))[:, :, None], (hh, c, LANES))
    st_decay = jnp.broadcast_to(jnp.exp(log_g[:, None] * (c - 1.0 - idx))[:, :, None],
                                (hh, c, LANES))
    ch_decay = jnp.exp(log_g * c)

    return pl.pallas_call(
        _retention_kernel,
        grid=(batch, hh),
        in_specs=[
            pl.BlockSpec(memory_space=pltpu.SMEM),
            pl.BlockSpec((seq, RET_DK), lambda b, h: (b, OFF_RQ // RET_DK + h)),
            pl.BlockSpec((seq, RET_DK), lambda b, h: (b, OFF_RK // RET_DK + h)),
            pl.BlockSpec((seq, RET_DV), lambda b, h: (b, OFF_RV // RET_DV + h)),
            pl.BlockSpec((None, c, c), lambda b, h: (h, 0, 0)),
            pl.BlockSpec((None, c, LANES), lambda b, h: (h, 0, 0)),
            pl.BlockSpec((None, c, LANES), lambda b, h: (h, 0, 0)),
        ],
        out_specs=pl.BlockSpec((seq, RET_DV), lambda b, h: (b, h)),
        out_shape=jax.ShapeDtypeStruct((batch * seq, RET_V_W), BF16),
        scratch_shapes=[pltpu.VMEM((RET_DK, RET_DV), F32)],
        name="retention",
        compiler_params=_params(("parallel", "parallel"), 40),
    )(ch_decay, z, z, z, intra, in_decay, st_decay)


def _diff_kernel(lam_init, slopes_ref, q_ref, k_ref, v_ref, lq1_ref, lk1_ref, lq2_ref, lk2_ref,
                 g_ref, o_ref, kaug_ref, qaug_ref, s_ref, mblk_ref, m_ref, l_ref, acc_ref):
    h = pl.program_id(1)
    seq = q_ref.shape[0]
    tq = DIFF_TQ
    tk = DIFF_TK
    d = DIFF_DK
    nq = seq // tq
    slope = slopes_ref[h]

    def build(j, carry):
        rows = pl.ds(pl.multiple_of(j * tk, tk), tk)
        pos = j * tk + lax.broadcasted_iota(jnp.int32, (tk, LANES), 0)
        lane = lax.broadcasted_iota(jnp.int32, (tk, LANES), 1)
        lo = pos & (POS_SPLIT - 1)
        kcols = jnp.where(lane == 0, (pos - lo).astype(F32) * slope,
                          jnp.where(lane == 1, lo.astype(F32) * slope,
                                    jnp.where(lane == 2, 1.0, 0.0))).astype(BF16)
        row = lax.broadcasted_iota(jnp.int32, (d, tq), 0)
        q0 = lax.convert_element_type(j * tq, F32)
        qrows = jnp.where(row < 2, 1.0, jnp.where(row == 2, -slope * q0, 0.0)).astype(BF16)
        for mp in range(2):
            kaug_ref[mp, rows, 0:d] = k_ref[rows, mp * d:(mp + 1) * d]
            kaug_ref[mp, rows, d:2 * d] = kcols
            qaug_ref[mp, j, 0:d] = q_ref[rows, mp * d:(mp + 1) * d].T
            qaug_ref[mp, j, d:2 * d] = qrows
        return carry

    lax.fori_loop(0, seq // tk, build, 0)

    lam = (jnp.exp(jnp.sum(lq1_ref[...] * lk1_ref[...], keepdims=True))
           - jnp.exp(jnp.sum(lq2_ref[...] * lk2_ref[...], keepdims=True)) + lam_init)

    def scores(qi, j, nblk, slot):
        krows = pl.ds(pl.multiple_of(j * tk, tk), nblk * tk)
        for mp in range(2):
            s = _dot(kaug_ref[mp, krows, :], qaug_ref[mp, qi])
            s_ref[slot, mp, 0:nblk * tk] = s
            mblk_ref[slot, mp] = jnp.max(s, axis=0, keepdims=True)

    def softmax_pv(j, nblk, slot, diag):
        nk = nblk * tk
        rows = pl.ds(pl.multiple_of(j * tk, tk), nk)
        v = v_ref[rows, :]
        for mp in range(2):
            s = s_ref[slot, mp, 0:nk]
            m_old = m_ref[mp]
            if diag:
                key = lax.broadcasted_iota(jnp.int32, (nk, tq), 0) - (nk - tk)
                qry = lax.broadcasted_iota(jnp.int32, (nk, tq), 1)
                s = jnp.where(key <= qry, s, -jnp.inf)
                m_blk = jnp.max(s, axis=0, keepdims=True)
            else:
                m_blk = mblk_ref[slot, mp]
            m_new = jnp.maximum(m_old, m_blk)
            alpha = jnp.exp(m_old - m_new)
            p = jnp.exp(s - m_new)
            l_ref[mp] = alpha * l_ref[mp] + jnp.sum(p, axis=0, keepdims=True)
            m_ref[mp] = m_new
            acc_ref[mp] = alpha * acc_ref[mp] + _dot_tn(v, p.astype(BF16))

    def step(qi, j, nblk, slot, diag, following):
        if following is not None:
            scores(qi, following[0], following[1], 1 - slot)
        softmax_pv(j, nblk, slot, diag)

    def tile(qi, carry):
        m_ref[...] = jnp.full_like(m_ref, -jnp.inf)
        l_ref[...] = jnp.zeros_like(l_ref)
        acc_ref[...] = jnp.zeros_like(acc_ref)

        def pair(t, c):
            step(qi, 4 * t, 2, 0, False, (4 * t + 2, 2))
            step(qi, 4 * t + 2, 2, 1, False, (4 * t + 4, 2))
            return c

        lax.fori_loop(0, qi // 4, pair, 0)
        rest = qi % 4

        @pl.when(rest == 0)
        def _():
            step(qi, qi, 1, 0, True, None)

        @pl.when(rest == 1)
        def _():
            step(qi, qi - 1, 2, 0, True, None)

        @pl.when(rest == 2)
        def _():
            step(qi, qi - 2, 2, 0, False, (qi, 1))
            step(qi, qi, 1, 1, True, None)

        @pl.when(rest == 3)
        def _():
            step(qi, qi - 3, 2, 0, False, (qi - 1, 2))
            step(qi, qi - 1, 2, 1, True, None)

        scores(jnp.minimum(qi + 1, nq - 1), 0, 2, 0)
        o = acc_ref[0] / l_ref[0] - lam * (acc_ref[1] / l_ref[1])
        o = o * lax.rsqrt(jnp.mean(o * o, axis=0, keepdims=True) + EPS)
        qrows = pl.ds(pl.multiple_of(qi * tq, tq), tq)
        o_ref[qrows, :] = (o.T * g_ref[...] * (1.0 - lam_init)).astype(o_ref.dtype)
        return carry

    scores(0, 0, 1, 0)
    lax.fori_loop(0, nq, tile, 0)


def _diff_attention(z, lq1, lk1, lq2, lk2, g_diff, lam_init, batch, seq):
    hh = DIFF_HEADS
    tq = DIFF_TQ
    assert DIFF_TQ == DIFF_TK and tq & (tq - 1) == 0 and seq % tq == 0
    assert 2 * tq <= seq <= POS_SPLIT * POS_SPLIT
    w = 2 * DIFF_DK
    slopes = [2.0 ** (-8.0 * (i + 1) / hh) for i in range(hh)]
    assert all(math.log2(s).is_integer() for s in slopes)
    vec = pl.BlockSpec((1, DIFF_DK), lambda b, h: (0, 0))
    return pl.pallas_call(
        functools.partial(_diff_kernel, lam_init),
        grid=(batch, hh),
        in_specs=[
            pl.BlockSpec(memory_space=pltpu.SMEM),
            pl.BlockSpec((seq, w), lambda b, h: (b, OFF_DQ // w + h)),
            pl.BlockSpec((seq, w), lambda b, h: (b, OFF_DK // w + h)),
            pl.BlockSpec((seq, DIFF_DV), lambda b, h: (b, OFF_DV // DIFF_DV + h)),
            vec, vec, vec, vec,
            pl.BlockSpec((1, DIFF_DV), lambda b, h: (0, h)),
        ],
        out_specs=pl.BlockSpec((seq, DIFF_DV), lambda b, h: (b, h)),
        out_shape=jax.ShapeDtypeStruct((batch * seq, DIFF_V_W), BF16),
        scratch_shapes=[pltpu.VMEM((2, seq, w), BF16), pltpu.VMEM((2, seq // tq, w, tq), BF16),
                        pltpu.VMEM((2, 2, 2 * DIFF_TK, tq), F32),
                        pltpu.VMEM((2, 2, 1, tq), F32),
                        pltpu.VMEM((2, 1, tq), F32), pltpu.VMEM((2, 1, tq), F32),
                        pltpu.VMEM((2, DIFF_DV, tq), F32)],
        name="diff_attention",
        compiler_params=_params(("parallel", "parallel"), 48),
    )(jnp.asarray(slopes, F32), z, z, z, lq1.reshape(1, -1), lk1.reshape(1, -1),
      lq2.reshape(1, -1), lk2.reshape(1, -1), g_diff.reshape(1, DIFF_V_W))


def _cross_kernel(q_ref, mk_ref, mv_ref, o_ref):
    dh = CROSS_DH
    for h in range(CROSS_HEADS):
        cols = slice(h * dh, (h + 1) * dh)
        s = _dot_nt(mk_ref[:, cols], q_ref[:, cols])
        p = jnp.exp(s - jnp.max(s, axis=0, keepdims=True))
        p = p * (1.0 / jnp.sum(p, axis=0, keepdims=True))
        o_ref[:, cols] = _dot_tn(p.astype(BF16), mv_ref[:, cols]).astype(o_ref.dtype)


def _cross_attention(z, kv, batch, seq):
    tq = CROSS_TQ
    nq = seq // tq
    return pl.pallas_call(
        _cross_kernel,
        grid=(batch, nq),
        in_specs=[
            pl.BlockSpec((tq, CROSS_W), lambda b, i: (b * nq + i, OFF_CQ // CROSS_W)),
            pl.BlockSpec((N_MEM, CROSS_W), lambda b, i: (b, 0)),
            pl.BlockSpec((N_MEM, CROSS_W), lambda b, i: (b, 1)),
        ],
        out_specs=pl.BlockSpec((tq, CROSS_W), lambda b, i: (b * nq + i, 0)),
        out_shape=jax.ShapeDtypeStruct((batch * seq, CROSS_W), BF16),
        name="cross_attention",
        compiler_params=_params(("parallel", "arbitrary"), 40),
    )(z, kv, kv)


def _merge_kernel(x_ref, ret_ref, rg_ref, da_ref, ca_ref, g0_ref, g1_ref, g2_ref,
                  gr_ref, wr_ref, wd_ref, wc_ref, wo_ref, o_ref):
    merged = _sigmoid(g1_ref[...].astype(F32)) * _dot(da_ref[...], wd_ref[...])
    merged += _sigmoid(g2_ref[...].astype(F32)) * _dot(ca_ref[...], wc_ref[...])
    y_ret = None
    for h in range(RET_HEADS):
        cols = slice(h * RET_DV, (h + 1) * RET_DV)
        o = ret_ref[:, cols].astype(F32)
        oc = o - jnp.mean(o, axis=-1, keepdims=True)
        y = _rms(oc, gr_ref[:, cols])
        rg = rg_ref[:, cols].astype(F32)
        part = _dot((rg * _sigmoid(rg) * y).astype(BF16), wr_ref[cols, :])
        y_ret = part if y_ret is None else y_ret + part
    merged += _sigmoid(g0_ref[...].astype(F32)) * y_ret
    o_ref[...] = x_ref[...] + _dot(merged.astype(BF16), wo_ref[...])


def _merge(x, ret, da, ca, z, g_ret, layer, w_ret_o, w_diff_o, w_cross_o, w_out):
    m, d = x.shape
    tm = MERGE_TM
    gate0 = OFF_GATE // d

    def rows(width):
        return pl.BlockSpec((tm, width), lambda i: (i, 0))

    def gate(n):
        return pl.BlockSpec((tm, d), lambda i: (i, gate0 + n))

    def weight(k):
        return _resident((None, k, d), lambda i: (layer, 0, 0))

    return pl.pallas_call(
        _merge_kernel,
        grid=(m // tm,),
        in_specs=[rows(d), rows(RET_V_W),
                  pl.BlockSpec((tm, RET_V_W), lambda i: (i, OFF_RG // RET_V_W)),
                  rows(DIFF_V_W), rows(CROSS_W), gate(0), gate(1), gate(2),
                  pl.BlockSpec((1, RET_V_W), lambda i: (0, 0)),
                  weight(RET_V_W), weight(DIFF_V_W), weight(CROSS_W), weight(d)],
        out_specs=rows(d),
        out_shape=jax.ShapeDtypeStruct((m, d), F32),
        name="merge",
        compiler_params=_params(("parallel",), 56),
    )(x, ret, z, da, ca, z, z, z, g_ret.reshape(1, RET_V_W), w_ret_o, w_diff_o, w_cross_o, w_out)


def _ffn_kernel(final_norm, x_ref, g_ref, wu_ref, wd_ref, gf_ref, o_ref):
    x = x_ref[...]
    h = _rms(x, g_ref[...]).astype(BF16)
    acc = x
    for f in range(D_FF // FFN_TF):
        cols = slice(f * FFN_TF, (f + 1) * FFN_TF)
        u = jnp.maximum(_dot(h, wu_ref[:, cols]), 0.0)
        acc = acc + _dot((u * u).astype(BF16), wd_ref[cols, :])
    if final_norm:
        acc = _rms(acc, gf_ref[...])
    o_ref[...] = acc


def _ffn(x, g_ffn, layer, w_up, w_down, g_final, final_norm):
    m, d = x.shape
    tm = FFN_TM
    vec = pl.BlockSpec((1, d), lambda i: (0, 0))
    return pl.pallas_call(
        functools.partial(_ffn_kernel, final_norm),
        grid=(m // tm,),
        in_specs=[pl.BlockSpec((tm, d), lambda i: (i, 0)), vec,
                  _resident((None, d, D_FF), lambda i: (layer, 0, 0)),
                  _resident((None, D_FF, d), lambda i: (layer, 0, 0)), vec],
        out_specs=pl.BlockSpec((tm, d), lambda i: (i, 0)),
        out_shape=jax.ShapeDtypeStruct((m, d), F32),
        name="ffn",
        compiler_params=_params(("parallel",), 56),
    )(x, g_ffn.reshape(1, d), w_up, w_down, g_final.reshape(1, d))


def _inproj_colscale():
    cs = jnp.ones((D_IN,), F32)
    cs = cs.at[OFF_RK:OFF_RK + RET_QK_W].set(RET_DK ** -0.5)
    cs = cs.at[OFF_DQ:OFF_DQ + DIFF_QK_W].set(DIFF_DK ** -0.5)
    cs = cs.at[OFF_CQ:OFF_CQ + CROSS_W].set(CROSS_DH ** -0.5)
    return cs


def kernel(x, mem, g_mix, w_in, g_ret, w_ret_o, lambda_q1, lambda_k1, lambda_q2, lambda_k2,
           g_diff, w_diff_o, g_mem, w_mem_kv, w_cross_o, w_out, g_ffn, w_up, w_down, g_final):
    batch, seq, d = x.shape
    xf = x.reshape(batch * seq, d)
    memf = mem.reshape(batch * mem.shape[1], d)
    colscale = _inproj_colscale()
    ones_kv = jnp.ones((2 * CROSS_W,), F32)
    w_in, w_mem_kv, w_ret_o, w_diff_o, w_cross_o, w_out, w_up, w_down = (
        w.astype(BF16) for w in (w_in, w_mem_kv, w_ret_o, w_diff_o, w_cross_o, w_out, w_up, w_down))
    for l in range(DEPTH):
        lam_init = 0.8 - 0.6 * math.exp(-0.3 * l)
        z = _inproj(xf, g_mix[l], w_in, l, colscale)
        ret = _retention(z, batch, seq)
        da = _diff_attention(z, lambda_q1[l], lambda_k1[l], lambda_q2[l], lambda_k2[l],
                             g_diff[l], lam_init, batch, seq)
        kv = _inproj(memf, g_mem[l], w_mem_kv, l, ones_kv)
        ca = _cross_attention(z, kv, batch, seq)
        xf = _merge(xf, ret, da, ca, z, g_ret[l], l, w_ret_o, w_diff_o, w_cross_o, w_out)
        xf = _ffn(xf, g_ffn[l], l, w_up, w_down, g_final, l == DEPTH - 1)
    return xf.reshape(batch, seq, d)
```

```python
import functools
import math

import jax
import jax.numpy as jnp
from jax import lax
from jax.experimental import pallas as pl
from jax.experimental.pallas import tpu as pltpu

F32 = jnp.float32
BF16 = jnp.bfloat16

D_MODEL = 1024
DEPTH = 2
N_MEM = 256
RET_HEADS = 4
RET_DK = D_MODEL // RET_HEADS
RET_DV = 2 * RET_DK
RET_KCHUNK = 256
DIFF_HEADS = 4
DIFF_DK = D_MODEL // (2 * DIFF_HEADS)
DIFF_DV = 2 * DIFF_DK
CROSS_HEADS = 4
CROSS_DH = D_MODEL // CROSS_HEADS
D_FF = 4 * D_MODEL
EPS = 1e-6

RET_QK_W = RET_HEADS * RET_DK
RET_V_W = RET_HEADS * RET_DV
DIFF_QK_W = DIFF_HEADS * 2 * DIFF_DK
DIFF_V_W = DIFF_HEADS * DIFF_DV
CROSS_W = CROSS_HEADS * CROSS_DH
GATE_W = 3 * D_MODEL
OFF_RQ = 0
OFF_RK = OFF_RQ + RET_QK_W
OFF_RV = OFF_RK + RET_QK_W
OFF_RG = OFF_RV + RET_V_W
OFF_DQ = OFF_RG + RET_V_W
OFF_DK = OFF_DQ + DIFF_QK_W
OFF_DV = OFF_DK + DIFF_QK_W
OFF_CQ = OFF_DV + DIFF_V_W
OFF_GATE = OFF_CQ + CROSS_W
D_IN = OFF_GATE + GATE_W

MIB = 1024 * 1024
LANES = 128
POS_SPLIT = 256

INPROJ_TM = 1024
INPROJ_TN = 3328
INPROJ_TC = 256
RET_UNROLL = 16
DIFF_TQ = 512
DIFF_TK = 512
CROSS_TQ = 4096
MERGE_TM = 512
FFN_TM = 1024
FFN_TF = 1024


def _params(semantics, vmem_mib):
    return pltpu.CompilerParams(dimension_semantics=semantics,
                                vmem_limit_bytes=vmem_mib * MIB)


def _resident(shape, index_map):
    return pl.BlockSpec(shape, index_map, pipeline_mode=pl.Buffered(1))


def _rms(x, g):
    ms = jnp.mean(x * x, axis=-1, keepdims=True)
    return x * lax.rsqrt(ms + EPS) * g


def _sigmoid(x):
    return 0.5 * jnp.tanh(0.5 * x) + 0.5


def _dot(a, b):
    return jnp.dot(a, b, preferred_element_type=F32)


def _dot_nt(a, b):
    return lax.dot_general(a, b, (((1,), (1,)), ((), ())), preferred_element_type=F32)


def _dot_tn(a, b):
    return lax.dot_general(a, b, (((0,), (0,)), ((), ())), preferred_element_type=F32)


def _inproj_kernel(x_ref, g_ref, w_ref, cs_ref, o_ref, h_ref):
    @pl.when(pl.program_id(1) == 0)
    def _():
        h_ref[...] = _rms(x_ref[...], g_ref[...]).astype(BF16)

    for c0 in range(0, o_ref.shape[1], INPROJ_TC):
        cols = slice(c0, c0 + INPROJ_TC)
        acc = _dot(h_ref[...], w_ref[:, cols])
        o_ref[:, cols] = (acc * cs_ref[:, cols]).astype(o_ref.dtype)


def _inproj(x, g, w, layer, colscale):
    m, k = x.shape
    n = w.shape[2]
    tm = min(INPROJ_TM, m)
    tn = min(INPROJ_TN, n)
    return pl.pallas_call(
        _inproj_kernel,
        grid=(m // tm, n // tn),
        in_specs=[
            pl.BlockSpec((tm, k), lambda i, j: (i, 0)),
            pl.BlockSpec((1, k), lambda i, j: (0, 0)),
            pl.BlockSpec((None, k, tn), lambda i, j: (layer, 0, j)),
            pl.BlockSpec((1, tn), lambda i, j: (0, j)),
        ],
        out_specs=pl.BlockSpec((tm, tn), lambda i, j: (i, j)),
        out_shape=jax.ShapeDtypeStruct((m, n), BF16),
        scratch_shapes=[pltpu.VMEM((tm, k), BF16)],
        name="inproj",
        compiler_params=_params(("parallel", "arbitrary"), 56),
    )(x, g.reshape(1, k), w, colscale.reshape(1, n))


def _retention_kernel(chd_ref, q_ref, k_ref, v_ref, intra_ref, ind_ref, std_ref, o_ref, state_ref):
    h = pl.program_id(1)
    state_ref[...] = jnp.zeros_like(state_ref)
    ch_decay = chd_ref[h]
    c = RET_KCHUNK

    def lanes(table, width):
        return jnp.concatenate([table] * (width // LANES), axis=1)

    def chunk(ci):
        rows = pl.ds(pl.multiple_of(ci * c, c), c)
        q = q_ref[rows, :]
        k = k_ref[rows, :]
        v = v_ref[rows, :]
        state = state_ref[...]
        scores = _dot_nt(q, k) * intra_ref[...]
        inner = _dot(scores.astype(BF16), v)
        qd = (q.astype(F32) * lanes(ind_ref[...], RET_DK)).astype(BF16)
        cross = _dot(qd, state.astype(BF16))
        kd = (k.astype(F32) * lanes(std_ref[...], RET_DK)).astype(BF16)
        state_ref[...] = ch_decay * state + _dot_tn(kd, v)
        o_ref[rows, :] = (inner + cross).astype(o_ref.dtype)

    def group(gi, carry):
        for ci in range(RET_UNROLL):
            chunk(gi * RET_UNROLL + ci)
        return carry

    lax.fori_loop(0, q_ref.shape[0] // (c * RET_UNROLL), group, 0)


def _retention(z, batch, seq):
    hh = RET_HEADS
    c = RET_KCHUNK
    assert seq % (c * RET_UNROLL) == 0
    log_g = jnp.log(1.0 - jnp.exp2(-5.0 - jnp.arange(hh, dtype=F32)))
    idx = jnp.arange(c, dtype=F32)
    rel = idx[:, None] - idx[None, :]
    intra = jnp.where(rel >= 0, jnp.exp(log_g[:, None, None] * jnp.maximum(rel, 0.0)), 0.0)
    in_decay = jnp.broadcast_to(jnp.exp(log_g[:, None] * (idx + 1.0))[:, :, None], (hh, c, LANES))
    st_decay = jnp.broadcast_to(jnp.exp(log_g[:, None] * (c - 1.0 - idx))[:, :, None],
                                (hh, c, LANES))
    ch_decay = jnp.exp(log_g * c)

    return pl.pallas_call(
        _retention_kernel,
        grid=(batch, hh),
        in_specs=[
            pl.BlockSpec(memory_space=pltpu.SMEM),
            pl.BlockSpec((seq, RET_DK), lambda b, h: (b, OFF_RQ // RET_DK + h)),
            pl.BlockSpec((seq, RET_DK), lambda b, h: (b, OFF_RK // RET_DK + h)),
            pl.BlockSpec((seq, RET_DV), lambda b, h: (b, OFF_RV // RET_DV + h)),
            pl.BlockSpec((None, c, c), lambda b, h: (h, 0, 0)),
            pl.BlockSpec((None, c, LANES), lambda b, h: (h, 0, 0)),
            pl.BlockSpec((None, c, LANES), lambda b, h: (h, 0, 0)),
        ],
        out_specs=pl.BlockSpec((seq, RET_DV), lambda b, h: (b, h)),
        out_shape=jax.ShapeDtypeStruct((batch * seq, RET_V_W), BF16),
        scratch_shapes=[pltpu.VMEM((RET_DK, RET_DV), F32)],
        name="retention",
        compiler_params=_params(("parallel", "parallel"), 40),
    )(ch_decay, z, z, z, intra, in_decay, st_decay)


def _diff_kernel(lam_init, slopes_ref, q_ref, k_ref, v_ref, lq1_ref, lk1_ref, lq2_ref, lk2_ref,
                 g_ref, o_ref, kaug_ref, qaug_ref, s_ref, mblk_ref, m_ref, l_ref, acc_ref):
    h = pl.program_id(1)
    seq = q_ref.shape[0]
    tq = DIFF_TQ
    tk = DIFF_TK
    d = DIFF_DK
    nq = seq // tq
    slope = slopes_ref[h]

    def build(j, carry):
        rows = pl.ds(pl.multiple_of(j * tk, tk), tk)
        pos = j * tk + lax.broadcasted_iota(jnp.int32, (tk, LANES), 0)
        lane = lax.broadcasted_iota(jnp.int32, (tk, LANES), 1)
        lo = pos & (POS_SPLIT - 1)
        kcols = jnp.where(lane == 0, (pos - lo).astype(F32) * slope,
                          jnp.where(lane == 1, lo.astype(F32) * slope,
                                    jnp.where(lane == 2, 1.0, 0.0))).astype(BF16)
        row = lax.broadcasted_iota(jnp.int32, (d, tq), 0)
        q0 = lax.convert_element_type(j * tq, F32)
        qrows = jnp.where(row < 2, 1.0, jnp.where(row == 2, -slope * q0, 0.0)).astype(BF16)
        for mp in range(2):
            kaug_ref[mp, rows, 0:d] = k_ref[rows, mp * d:(mp + 1) * d]
            kaug_ref[mp, rows, d:2 * d] = kcols
            qaug_ref[mp, j, 0:d] = q_ref[rows, mp * d:(mp + 1) * d].T
            qaug_ref[mp, j, d:2 * d] = qrows
        return carry

    lax.fori_loop(0, seq // tk, build, 0)

    lam = (jnp.exp(jnp.sum(lq1_ref[...] * lk1_ref[...], keepdims=True))
           - jnp.exp(jnp.sum(lq2_ref[...] * lk2_ref[...], keepdims=True)) + lam_init)

    def scores(qi, j, nblk, slot):
        krows = pl.ds(pl.multiple_of(j * tk, tk), nblk * tk)
        for mp in range(2):
            s = _dot(kaug_ref[mp, krows, :], qaug_ref[mp, qi])
            s_ref[slot, mp, 0:nblk * tk] = s
            mblk_ref[slot, mp] = jnp.max(s, axis=0, keepdims=True)

    def softmax_pv(j, nblk, slot, diag):
        nk = nblk * tk
        rows = pl.ds(pl.multiple_of(j * tk, tk), nk)
        v = v_ref[rows, :]
        for mp in range(2):
            s = s_ref[slot, mp, 0:nk]
            m_old = m_ref[mp]
            if diag:
                key = lax.broadcasted_iota(jnp.int32, (nk, tq), 0) - (nk - tk)
                qry = lax.broadcasted_iota(jnp.int32, (nk, tq), 1)
                s = jnp.where(key <= qry, s, -jnp.inf)
                m_blk = jnp.max(s, axis=0, keepdims=True)
            else:
                m_blk = mblk_ref[slot, mp]
            m_new = jnp.maximum(m_old, m_blk)
            alpha = jnp.exp(m_old - m_new)
            p = jnp.exp(s - m_new)
            l_ref[mp] = alpha * l_ref[mp] + jnp.sum(p, axis=0, keepdims=True)
            m_ref[mp] = m_new
            acc_ref[mp] = alpha * acc_ref[mp] + _dot_tn(v, p.astype(BF16))

    def step(qi, j, nblk, slot, diag, following):
        if following is not None:
            scores(qi, following[0], following[1], 1 - slot)
        softmax_pv(j, nblk, slot, diag)

    def tile(qi, carry):
        m_ref[...] = jnp.full_like(m_ref, -jnp.inf)
        l_ref[...] = jnp.zeros_like(l_ref)
        acc_ref[...] = jnp.zeros_like(acc_ref)

        def pair(t, c):
            step(qi, 4 * t, 2, 0, False, (4 * t + 2, 2))
            step(qi, 4 * t + 2, 2, 1, False, (4 * t + 4, 2))
            return c

        lax.fori_loop(0, qi // 4, pair, 0)
        rest = qi % 4

        @pl.when(rest == 0)
        def _():
            step(qi, qi, 1, 0, True, None)

        @pl.when(rest == 1)
        def _():
            step(qi, qi - 1, 2, 0, True, None)

        @pl.when(rest == 2)
        def _():
            step(qi, qi - 2, 2, 0, False, (qi, 1))
            step(qi, qi, 1, 1, True, None)

        @pl.when(rest == 3)
        def _():
            step(qi, qi - 3, 2, 0, False, (qi - 1, 2))
            step(qi, qi - 1, 2, 1, True, None)

        scores(jnp.minimum(qi + 1, nq - 1), 0, 2, 0)
        o = acc_ref[0] / l_ref[0] - lam * (acc_ref[1] / l_ref[1])
        o = o * lax.rsqrt(jnp.mean(o * o, axis=0, keepdims=True) + EPS)
        qrows = pl.ds(pl.multiple_of(qi * tq, tq), tq)
        o_ref[qrows, :] = (o.T * g_ref[...] * (1.0 - lam_init)).astype(o_ref.dtype)
        return carry

    scores(0, 0, 1, 0)
    lax.fori_loop(0, nq, tile, 0)


def _diff_attention(z, lq1, lk1, lq2, lk2, g_diff, lam_init, batch, seq):
    hh = DIFF_HEADS
    tq = DIFF_TQ
    assert DIFF_TQ == DIFF_TK and tq & (tq - 1) == 0 and seq % tq == 0
    assert 2 * tq <= seq <= POS_SPLIT * POS_SPLIT
    w = 2 * DIFF_DK
    slopes = [2.0 ** (-8.0 * (i + 1) / hh) for i in range(hh)]
    assert all(math.log2(s).is_integer() for s in slopes)
    vec = pl.BlockSpec((1, DIFF_DK), lambda b, h: (0, 0))
    return pl.pallas_call(
        functools.partial(_diff_kernel, lam_init),
        grid=(batch, hh),
        in_specs=[
            pl.BlockSpec(memory_space=pltpu.SMEM),
            pl.BlockSpec((seq, w), lambda b, h: (b, OFF_DQ // w + h)),
            pl.BlockSpec((seq, w), lambda b, h: (b, OFF_DK // w + h)),
            pl.BlockSpec((seq, DIFF_DV), lambda b, h: (b, OFF_DV // DIFF_DV + h)),
            vec, vec, vec, vec,
            pl.BlockSpec((1, DIFF_DV), lambda b, h: (0, h)),
        ],
        out_specs=pl.BlockSpec((seq, DIFF_DV), lambda b, h: (b, h)),
        out_shape=jax.ShapeDtypeStruct((batch * seq, DIFF_V_W), BF16),
        scratch_shapes=[pltpu.VMEM((2, seq, w), BF16), pltpu.VMEM((2, seq // tq, w, tq), BF16),
                        pltpu.VMEM((2, 2, 2 * DIFF_TK, tq), F32),
                        pltpu.VMEM((2, 2, 1, tq), F32),
                        pltpu.VMEM((2, 1, tq), F32), pltpu.VMEM((2, 1, tq), F32),
                        pltpu.VMEM((2, DIFF_DV, tq), F32)],
        name="diff_attention",
        compiler_params=_params(("parallel", "parallel"), 48),
    )(jnp.asarray(slopes, F32), z, z, z, lq1.reshape(1, -1), lk1.reshape(1, -1),
      lq2.reshape(1, -1), lk2.reshape(1, -1), g_diff.reshape(1, DIFF_V_W))


def _cross_kernel(q_ref, mk_ref, mv_ref, o_ref):
    dh = CROSS_DH
    for h in range(CROSS_HEADS):
        cols = slice(h * dh, (h + 1) * dh)
        s = _dot_nt(mk_ref[:, cols], q_ref[:, cols])
        p = jnp.exp(s - jnp.max(s, axis=0, keepdims=True))
        p = p * (1.0 / jnp.sum(p, axis=0, keepdims=True))
        o_ref[:, cols] = _dot_tn(p.astype(BF16), mv_ref[:, cols]).astype(o_ref.dtype)


def _cross_attention(z, kv, batch, seq):
    tq = CROSS_TQ
    nq = seq // tq
    return pl.pallas_call(
        _cross_kernel,
        grid=(batch, nq),
        in_specs=[
            pl.BlockSpec((tq, CROSS_W), lambda b, i: (b * nq + i, OFF_CQ // CROSS_W)),
            pl.BlockSpec((N_MEM, CROSS_W), lambda b, i: (b, 0)),
            pl.BlockSpec((N_MEM, CROSS_W), lambda b, i: (b, 1)),
        ],
        out_specs=pl.BlockSpec((tq, CROSS_W), lambda b, i: (b * nq + i, 0)),
        out_shape=jax.ShapeDtypeStruct((batch * seq, CROSS_W), BF16),
        name="cross_attention",
        compiler_params=_params(("parallel", "arbitrary"), 40),
    )(z, kv, kv)


def _merge_kernel(x_ref, ret_ref, rg_ref, da_ref, ca_ref, g0_ref, g1_ref, g2_ref,
                  gr_ref, wr_ref, wd_ref, wc_ref, wo_ref, o_ref):
    merged = _sigmoid(g1_ref[...].astype(F32)) * _dot(da_ref[...], wd_ref[...])
    merged += _sigmoid(g2_ref[...].astype(F32)) * _dot(ca_ref[...], wc_ref[...])
    y_ret = None
    for h in range(RET_HEADS):
        cols = slice(h * RET_DV, (h + 1) * RET_DV)
        o = ret_ref[:, cols].astype(F32)
        oc = o - jnp.mean(o, axis=-1, keepdims=True)
        y = _rms(oc, gr_ref[:, cols])
        rg = rg_ref[:, cols].astype(F32)
        part = _dot((rg * _sigmoid(rg) * y).astype(BF16), wr_ref[cols, :])
        y_ret = part if y_ret is None else y_ret + part
    merged += _sigmoid(g0_ref[...].astype(F32)) * y_ret
    o_ref[...] = x_ref[...] + _dot(merged.astype(BF16), wo_ref[...])


def _merge(x, ret, da, ca, z, g_ret, layer, w_ret_o, w_diff_o, w_cross_o, w_out):
    m, d = x.shape
    tm = MERGE_TM
    gate0 = OFF_GATE // d

    def rows(width):
        return pl.BlockSpec((tm, width), lambda i: (i, 0))

    def gate(n):
        return pl.BlockSpec((tm, d), lambda i: (i, gate0 + n))

    def weight(k):
        return _resident((None, k, d), lambda i: (layer, 0, 0))

    return pl.pallas_call(
        _merge_kernel,
        grid=(m // tm,),
        in_specs=[rows(d), rows(RET_V_W),
                  pl.BlockSpec((tm, RET_V_W), lambda i: (i, OFF_RG // RET_V_W)),
                  rows(DIFF_V_W), rows(CROSS_W), gate(0), gate(1), gate(2),
                  pl.BlockSpec((1, RET_V_W), lambda i: (0, 0)),
                  weight(RET_V_W), weight(DIFF_V_W), weight(CROSS_W), weight(d)],
        out_specs=rows(d),
        out_shape=jax.ShapeDtypeStruct((m, d), F32),
        name="merge",
        compiler_params=_params(("parallel",), 56),
    )(x, ret, z, da, ca, z, z, z, g_ret.reshape(1, RET_V_W), w_ret_o, w_diff_o, w_cross_o, w_out)


def _ffn_kernel(final_norm, x_ref, g_ref, wu_ref, wd_ref, gf_ref, o_ref):
    x = x_ref[...]
    h = _rms(x, g_ref[...]).astype(BF16)
    acc = x
    for f in range(D_FF // FFN_TF):
        cols = slice(f * FFN_TF, (f + 1) * FFN_TF)
        u = jnp.maximum(_dot(h, wu_ref[:, cols]), 0.0)
        acc = acc + _dot((u * u).astype(BF16), wd_ref[cols, :])
    if final_norm:
        acc = _rms(acc, gf_ref[...])
    o_ref[...] = acc


def _ffn(x, g_ffn, layer, w_up, w_down, g_final, final_norm):
    m, d = x.shape
    tm = FFN_TM
    vec = pl.BlockSpec((1, d), lambda i: (0, 0))
    return pl.pallas_call(
        functools.partial(_ffn_kernel, final_norm),
        grid=(m // tm,),
        in_specs=[pl.BlockSpec((tm, d), lambda i: (i, 0)), vec,
                  _resident((None, d, D_FF), lambda i: (layer, 0, 0)),
                  _resident((None, D_FF, d), lambda i: (layer, 0, 0)), vec],
        out_specs=pl.BlockSpec((tm, d), lambda i: (i, 0)),
        out_shape=jax.ShapeDtypeStruct((m, d), F32),
        name="ffn",
        compiler_params=_params(("parallel",), 56),
    )(x, g_ffn.reshape(1, d), w_up, w_down, g_final.reshape(1, d))


def _inproj_colscale():
    cs = jnp.ones((D_IN,), F32)
    cs = cs.at[OFF_RK:OFF_RK + RET_QK_W].set(RET_DK ** -0.5)
    cs = cs.at[OFF_DQ:OFF_DQ + DIFF_QK_W].set(DIFF_DK ** -0.5)
    cs = cs.at[OFF_CQ:OFF_CQ + CROSS_W].set(CROSS_DH ** -0.5)
    return cs


def kernel(x, mem, g_mix, w_in, g_ret, w_ret_o, lambda_q1, lambda_k1, lambda_q2, lambda_k2,
           g_diff, w_diff_o, g_mem, w_mem_kv, w_cross_o, w_out, g_ffn, w_up, w_down, g_final):
    batch, seq, d = x.shape
    xf = x.reshape(batch * seq, d)
    memf = mem.reshape(batch * mem.shape[1], d)
    colscale = _inproj_colscale()
    ones_kv = jnp.ones((2 * CROSS_W,), F32)
    w_in, w_mem_kv, w_ret_o, w_diff_o, w_cross_o, w_out, w_up, w_down = (
        w.astype(BF16) for w in (w_in, w_mem_kv, w_ret_o, w_diff_o, w_cross_o, w_out, w_up, w_down))
    for l in range(DEPTH):
        lam_init = 0.8 - 0.6 * math.exp(-0.3 * l)
        z = _inproj(xf, g_mix[l], w_in, l, colscale)
        ret = _retention(z, batch, seq)
        da = _diff_attention(z, lambda_q1[l], lambda_k1[l], lambda_q2[l], lambda_k2[l],
                             g_diff[l], lam_init, batch, seq)
        kv = _inproj(memf, g_mem[l], w_mem_kv, l, ones_kv)
        ca = _cross_attention(z, kv, batch, seq)
        xf = _merge(xf, ret, da, ca, z, g_ret[l], l, w_ret_o, w_diff_o, w_cross_o, w_out)
        xf = _ffn(xf, g_ffn[l], l, w_up, w_down, g_final, l == DEPTH - 1)
    return xf.reshape(batch, seq, d)
```

```python
import functools
import math

import jax
import jax.numpy as jnp
from jax import lax
from jax.experimental import pallas as pl
from jax.experimental.pallas import tpu as pltpu

F32 = jnp.float32
BF16 = jnp.bfloat16

D_MODEL = 1024
DEPTH = 2
N_MEM = 256
RET_HEADS = 4
RET_DK = D_MODEL // RET_HEADS
RET_DV = 2 * RET_DK
RET_KCHUNK = 256
DIFF_HEADS = 4
DIFF_DK = D_MODEL // (2 * DIFF_HEADS)
DIFF_DV = 2 * DIFF_DK
CROSS_HEADS = 4
CROSS_DH = D_MODEL // CROSS_HEADS
D_FF = 4 * D_MODEL
EPS = 1e-6

RET_QK_W = RET_HEADS * RET_DK
RET_V_W = RET_HEADS * RET_DV
DIFF_QK_W = DIFF_HEADS * 2 * DIFF_DK
DIFF_V_W = DIFF_HEADS * DIFF_DV
CROSS_W = CROSS_HEADS * CROSS_DH
GATE_W = 3 * D_MODEL
OFF_RQ = 0
OFF_RK = OFF_RQ + RET_QK_W
OFF_RV = OFF_RK + RET_QK_W
OFF_RG = OFF_RV + RET_V_W
OFF_DQ = OFF_RG + RET_V_W
OFF_DK = OFF_DQ + DIFF_QK_W
OFF_DV = OFF_DK + DIFF_QK_W
OFF_CQ = OFF_DV + DIFF_V_W
OFF_GATE = OFF_CQ + CROSS_W
D_IN = OFF_GATE + GATE_W

MIB = 1024 * 1024
LANES = 128
POS_SPLIT = 256

INPROJ_TM = 1024
INPROJ_TN = 3328
INPROJ_TC = 256
RET_UNROLL = 16
DIFF_TQ = 512
DIFF_TK = 512
CROSS_TQ = 4096
MERGE_TM = 512
FFN_TM = 1024
FFN_TF = 1024


def _params(semantics, vmem_mib):
    return pltpu.CompilerParams(dimension_semantics=semantics,
                                vmem_limit_bytes=vmem_mib * MIB)


def _resident(shape, index_map):
    return pl.BlockSpec(shape, index_map, pipeline_mode=pl.Buffered(1))


def _rms(x, g):
    ms = jnp.mean(x * x, axis=-1, keepdims=True)
    return x * lax.rsqrt(ms + EPS) * g


def _sigmoid(x):
    return 0.5 * jnp.tanh(0.5 * x) + 0.5


def _dot(a, b):
    return jnp.dot(a, b, preferred_element_type=F32)


def _dot_nt(a, b):
    return lax.dot_general(a, b, (((1,), (1,)), ((), ())), preferred_element_type=F32)


def _dot_tn(a, b):
    return lax.dot_general(a, b, (((0,), (0,)), ((), ())), preferred_element_type=F32)


def _inproj_kernel(x_ref, g_ref, w_ref, cs_ref, o_ref, h_ref):
    @pl.when(pl.program_id(1) == 0)
    def _():
        h_ref[...] = _rms(x_ref[...], g_ref[...]).astype(BF16)

    for c0 in range(0, o_ref.shape[1], INPROJ_TC):
        cols = slice(c0, c0 + INPROJ_TC)
        acc = _dot(h_ref[...], w_ref[:, cols])
        o_ref[:, cols] = (acc * cs_ref[:, cols]).astype(o_ref.dtype)


def _inproj(x, g, w, layer, colscale):
    m, k = x.shape
    n = w.shape[2]
    tm = min(INPROJ_TM, m)
    tn = min(INPROJ_TN, n)
    return pl.pallas_call(
        _inproj_kernel,
        grid=(m // tm, n // tn),
        in_specs=[
            pl.BlockSpec((tm, k), lambda i, j: (i, 0)),
            pl.BlockSpec((1, k), lambda i, j: (0, 0)),
            pl.BlockSpec((None, k, tn), lambda i, j: (layer, 0, j)),
            pl.BlockSpec((1, tn), lambda i, j: (0, j)),
        ],
        out_specs=pl.BlockSpec((tm, tn), lambda i, j: (i, j)),
        out_shape=jax.ShapeDtypeStruct((m, n), BF16),
        scratch_shapes=[pltpu.VMEM((tm, k), BF16)],
        name="inproj",
        compiler_params=_params(("parallel", "arbitrary"), 56),
    )(x, g.reshape(1, k), w, colscale.reshape(1, n))


def _retention_kernel(chd_ref, q_ref, k_ref, v_ref, intra_ref, ind_ref, std_ref, o_ref, state_ref):
    h = pl.program_id(1)
    state_ref[...] = jnp.zeros_like(state_ref)
    ch_decay = chd_ref[h]
    c = RET_KCHUNK

    def lanes(table, width):
        return jnp.concatenate([table] * (width // LANES), axis=1)

    def chunk(ci):
        rows = pl.ds(pl.multiple_of(ci * c, c), c)
        q = q_ref[rows, :]
        k = k_ref[rows, :]
        v = v_ref[rows, :]
        state = state_ref[...]
        scores = _dot_nt(q, k) * intra_ref[...]
        inner = _dot(scores.astype(BF16), v)
        qd = (q.astype(F32) * lanes(ind_ref[...], RET_DK)).astype(BF16)
        cross = _dot(qd, state.astype(BF16))
        kd = (k.astype(F32) * lanes(std_ref[...], RET_DK)).astype(BF16)
        state_ref[...] = ch_decay * state + _dot_tn(kd, v)
        o_ref[rows, :] = (inner + cross).astype(o_ref.dtype)

    def group(gi, carry):
        for ci in range(RET_UNROLL):
            chunk(gi * RET_UNROLL + ci)
        return carry

    lax.fori_loop(0, q_ref.shape[0] // (c * RET_UNROLL), group, 0)


def _retention(z, batch, seq):
    hh = RET_HEADS
    c = RET_KCHUNK
    assert seq % (c * RET_UNROLL) == 0
    log_g = jnp.log(1.0 - jnp.exp2(-5.0 - jnp.arange(hh, dtype=F32)))
    idx = jnp.arange(c, dtype=F32)
    rel = idx[:, None] - idx[None, :]
    intra = jnp.where(rel >= 0, jnp.exp(log_g[:, None, None] * jnp.maximum(rel, 0.0)), 0.0)
    in_decay = jnp.broadcast_to(jnp.exp(log_g[:, None] * (idx + 1.0))[:, :, None], (hh, c, LANES))
    st_decay = jnp.broadcast_to(jnp.exp(log_g[:, None] * (c - 1.0 - idx))[:, :, None],
                                (hh, c, LANES))
    ch_decay = jnp.exp(log_g * c)

    return pl.pallas_call(
        _retention_kernel,
        grid=(batch, hh),
        in_specs=[
            pl.BlockSpec(memory_space=pltpu.SMEM),
            pl.BlockSpec((seq, RET_DK), lambda b, h: (b, OFF_RQ // RET_DK + h)),
            pl.BlockSpec((seq, RET_DK), lambda b, h: (b, OFF_RK // RET_DK + h)),
            pl.BlockSpec((seq, RET_DV), lambda b, h: (b, OFF_RV // RET_DV + h)),
            pl.BlockSpec((None, c, c), lambda b, h: (h, 0, 0)),
            pl.BlockSpec((None, c, LANES), lambda b, h: (h, 0, 0)),
            pl.BlockSpec((None, c, LANES), lambda b, h: (h, 0, 0)),
        ],
        out_specs=pl.BlockSpec((seq, RET_DV), lambda b, h: (b, h)),
        out_shape=jax.ShapeDtypeStruct((batch * seq, RET_V_W), BF16),
        scratch_shapes=[pltpu.VMEM((RET_DK, RET_DV), F32)],
        name="retention",
        compiler_params=_params(("parallel", "parallel"), 40),
    )(ch_decay, z, z, z, intra, in_decay, st_decay)


def _diff_kernel(lam_init, slopes_ref, q_ref, k_ref, v_ref, lq1_ref, lk1_ref, lq2_ref, lk2_ref,
                 g_ref, o_ref, kaug_ref, qaug_ref, s_ref, mblk_ref, m_ref, l_ref, acc_ref):
    h = pl.program_id(1)
    seq = q_ref.shape[0]
    tq = DIFF_TQ
    tk = DIFF_TK
    d = DIFF_DK
    nq = seq // tq
    slope = slopes_ref[h]

    def build(j, carry):
        rows = pl.ds(pl.multiple_of(j * tk, tk), tk)
        pos = j * tk + lax.broadcasted_iota(jnp.int32, (tk, LANES), 0)
        lane = lax.broadcasted_iota(jnp.int32, (tk, LANES), 1)
        lo = pos & (POS_SPLIT - 1)
        kcols = jnp.where(lane == 0, (pos - lo).astype(F32) * slope,
                          jnp.where(lane == 1, lo.astype(F32) * slope,
                                    jnp.where(lane == 2, 1.0, 0.0))).astype(BF16)
        row = lax.broadcasted_iota(jnp.int32, (d, tq), 0)
        q0 = lax.convert_element_type(j * tq, F32)
        qrows = jnp.where(row < 2, 1.0, jnp.where(row == 2, -slope * q0, 0.0)).astype(BF16)
        for mp in range(2):
            kaug_ref[mp, rows, 0:d] = k_ref[rows, mp * d:(mp + 1) * d]
            kaug_ref[mp, rows, d:2 * d] = kcols
            qaug_ref[mp, j, 0:d] = q_ref[rows, mp * d:(mp + 1) * d].T
            qaug_ref[mp, j, d:2 * d] = qrows
        return carry

    lax.fori_loop(0, seq // tk, build, 0)

    lam = (jnp.exp(jnp.sum(lq1_ref[...] * lk1_ref[...], keepdims=True))
           - jnp.exp(jnp.sum(lq2_ref[...] * lk2_ref[...], keepdims=True)) + lam_init)

    def by_query(row):
        col = jnp.broadcast_to(row, (LANES, tq)).T
        return jnp.concatenate([col] * (DIFF_DV // LANES), axis=1)

    def scores(qi, j, nblk, slot):
        krows = pl.ds(pl.multiple_of(j * tk, tk), nblk * tk)
        for mp in range(2):
            s = _dot(kaug_ref[mp, krows, :], qaug_ref[mp, qi])
            s_ref[slot, mp, 0:nblk * tk] = s
            mblk_ref[slot, mp] = jnp.max(s, axis=0, keepdims=True)

    def softmax_pv(j, nblk, slot, diag):
        nk = nblk * tk
        rows = pl.ds(pl.multiple_of(j * tk, tk), nk)
        v = v_ref[rows, :]
        for mp in range(2):
            s = s_ref[slot, mp, 0:nk]
            m_old = m_ref[mp]
            if diag:
                key = lax.broadcasted_iota(jnp.int32, (nk, tq), 0) - (nk - tk)
                qry = lax.broadcasted_iota(jnp.int32, (nk, tq), 1)
                s = jnp.where(key <= qry, s, -jnp.inf)
                m_blk = jnp.max(s, axis=0, keepdims=True)
            else:
                m_blk = mblk_ref[slot, mp]
            m_new = jnp.maximum(m_old, m_blk)
            alpha = jnp.exp(m_old - m_new)
            p = jnp.exp(s - m_new)
            l_ref[mp] = alpha * l_ref[mp] + jnp.sum(p, axis=0, keepdims=True)
            m_ref[mp] = m_new
            acc_ref[mp] = by_query(alpha) * acc_ref[mp] + _dot_tn(p.astype(BF16), v)

    def step(qi, j, nblk, slot, diag, following):
        if following is not None:
            scores(qi, following[0], following[1], 1 - slot)
        softmax_pv(j, nblk, slot, diag)

    def tile(qi, carry):
        m_ref[...] = jnp.full_like(m_ref, -jnp.inf)
        l_ref[...] = jnp.zeros_like(l_ref)
        acc_ref[...] = jnp.zeros_like(acc_ref)

        def pair(t, c):
            step(qi, 4 * t, 2, 0, False, (4 * t + 2, 2))
            step(qi, 4 * t + 2, 2, 1, False, (4 * t + 4, 2))
            return c

        lax.fori_loop(0, qi // 4, pair, 0)
        rest = qi % 4

        @pl.when(rest == 0)
        def _():
            step(qi, qi, 1, 0, True, None)

        @pl.when(rest == 1)
        def _():
            step(qi, qi - 1, 2, 0, True, None)

        @pl.when(rest == 2)
        def _():
            step(qi, qi - 2, 2, 0, False, (qi, 1))
            step(qi, qi, 1, 1, True, None)

        @pl.when(rest == 3)
        def _():
            step(qi, qi - 3, 2, 0, False, (qi - 1, 2))
            step(qi, qi - 1, 2, 1, True, None)

        scores(jnp.minimum(qi + 1, nq - 1), 0, 2, 0)
        o = (acc_ref[0] * by_query(1.0 / l_ref[0])
             - lam * (acc_ref[1] * by_query(1.0 / l_ref[1])))
        qrows = pl.ds(pl.multiple_of(qi * tq, tq), tq)
        o_ref[qrows, :] = (_rms(o, g_ref[...]) * (1.0 - lam_init)).astype(o_ref.dtype)
        return carry

    scores(0, 0, 1, 0)
    lax.fori_loop(0, nq, tile, 0)


def _diff_attention(z, lq1, lk1, lq2, lk2, g_diff, lam_init, batch, seq):
    hh = DIFF_HEADS
    tq = DIFF_TQ
    assert DIFF_TQ == DIFF_TK and tq & (tq - 1) == 0 and seq % tq == 0
    assert 2 * tq <= seq <= POS_SPLIT * POS_SPLIT
    w = 2 * DIFF_DK
    slopes = [2.0 ** (-8.0 * (i + 1) / hh) for i in range(hh)]
    assert all(math.log2(s).is_integer() for s in slopes)
    vec = pl.BlockSpec((1, DIFF_DK), lambda b, h: (0, 0))
    return pl.pallas_call(
        functools.partial(_diff_kernel, lam_init),
        grid=(batch, hh),
        in_specs=[
            pl.BlockSpec(memory_space=pltpu.SMEM),
            pl.BlockSpec((seq, w), lambda b, h: (b, OFF_DQ // w + h)),
            pl.BlockSpec((seq, w), lambda b, h: (b, OFF_DK // w + h)),
            pl.BlockSpec((seq, DIFF_DV), lambda b, h: (b, OFF_DV // DIFF_DV + h)),
            vec, vec, vec, vec,
            pl.BlockSpec((1, DIFF_DV), lambda b, h: (0, h)),
        ],
        out_specs=pl.BlockSpec((seq, DIFF_DV), lambda b, h: (b, h)),
        out_shape=jax.ShapeDtypeStruct((batch * seq, DIFF_V_W), BF16),
        scratch_shapes=[pltpu.VMEM((2, seq, w), BF16), pltpu.VMEM((2, seq // tq, w, tq), BF16),
                        pltpu.VMEM((2, 2, 2 * DIFF_TK, tq), F32),
                        pltpu.VMEM((2, 2, 1, tq), F32),
                        pltpu.VMEM((2, 1, tq), F32), pltpu.VMEM((2, 1, tq), F32),
                        pltpu.VMEM((2, tq, DIFF_DV), F32)],
        name="diff_attention",
        compiler_params=_params(("parallel", "parallel"), 48),
    )(jnp.asarray(slopes, F32), z, z, z, lq1.reshape(1, -1), lk1.reshape(1, -1),
      lq2.reshape(1, -1), lk2.reshape(1, -1), g_diff.reshape(1, DIFF_V_W))


def _cross_kernel(q_ref, mk_ref, mv_ref, o_ref):
    dh = CROSS_DH
    for h in range(CROSS_HEADS):
        cols = slice(h * dh, (h + 1) * dh)
        s = _dot_nt(mk_ref[:, cols], q_ref[:, cols])
        p = jnp.exp(s - jnp.max(s, axis=0, keepdims=True))
        p = p * (1.0 / jnp.sum(p, axis=0, keepdims=True))
        o_ref[:, cols] = _dot_tn(p.astype(BF16), mv_ref[:, cols]).astype(o_ref.dtype)


def _cross_attention(z, kv, batch, seq):
    tq = CROSS_TQ
    nq = seq // tq
    return pl.pallas_call(
        _cross_kernel,
        grid=(batch, nq),
        in_specs=[
            pl.BlockSpec((tq, CROSS_W), lambda b, i: (b * nq + i, OFF_CQ // CROSS_W)),
            pl.BlockSpec((N_MEM, CROSS_W), lambda b, i: (b, 0)),
            pl.BlockSpec((N_MEM, CROSS_W), lambda b, i: (b, 1)),
        ],
        out_specs=pl.BlockSpec((tq, CROSS_W), lambda b, i: (b * nq + i, 0)),
        out_shape=jax.ShapeDtypeStruct((batch * seq, CROSS_W), BF16),
        name="cross_attention",
        compiler_params=_params(("parallel", "arbitrary"), 40),
    )(z, kv, kv)


def _merge_kernel(x_ref, ret_ref, rg_ref, da_ref, ca_ref, g0_ref, g1_ref, g2_ref,
                  gr_ref, wr_ref, wd_ref, wc_ref, wo_ref, o_ref):
    merged = _sigmoid(g1_ref[...].astype(F32)) * _dot(da_ref[...], wd_ref[...])
    merged += _sigmoid(g2_ref[...].astype(F32)) * _dot(ca_ref[...], wc_ref[...])
    y_ret = None
    for h in range(RET_HEADS):
        cols = slice(h * RET_DV, (h + 1) * RET_DV)
        o = ret_ref[:, cols].astype(F32)
        oc = o - jnp.mean(o, axis=-1, keepdims=True)
        y = _rms(oc, gr_ref[:, cols])
        rg = rg_ref[:, cols].astype(F32)
        part = _dot((rg * _sigmoid(rg) * y).astype(BF16), wr_ref[cols, :])
        y_ret = part if y_ret is None else y_ret + part
    merged += _sigmoid(g0_ref[...].astype(F32)) * y_ret
    o_ref[...] = x_ref[...] + _dot(merged.astype(BF16), wo_ref[...])


def _merge(x, ret, da, ca, z, g_ret, layer, w_ret_o, w_diff_o, w_cross_o, w_out):
    m, d = x.shape
    tm = MERGE_TM
    gate0 = OFF_GATE // d

    def rows(width):
        return pl.BlockSpec((tm, width), lambda i: (i, 0))

    def gate(n):
        return pl.BlockSpec((tm, d), lambda i: (i, gate0 + n))

    def weight(k):
        return _resident((None, k, d), lambda i: (layer, 0, 0))

    return pl.pallas_call(
        _merge_kernel,
        grid=(m // tm,),
        in_specs=[rows(d), rows(RET_V_W),
                  pl.BlockSpec((tm, RET_V_W), lambda i: (i, OFF_RG // RET_V_W)),
                  rows(DIFF_V_W), rows(CROSS_W), gate(0), gate(1), gate(2),
                  pl.BlockSpec((1, RET_V_W), lambda i: (0, 0)),
                  weight(RET_V_W), weight(DIFF_V_W), weight(CROSS_W), weight(d)],
        out_specs=rows(d),
        out_shape=jax.ShapeDtypeStruct((m, d), F32),
        name="merge",
        compiler_params=_params(("parallel",), 56),
    )(x, ret, z, da, ca, z, z, z, g_ret.reshape(1, RET_V_W), w_ret_o, w_diff_o, w_cross_o, w_out)


def _ffn_kernel(final_norm, x_ref, g_ref, wu_ref, wd_ref, gf_ref, o_ref):
    x = x_ref[...]
    h = _rms(x, g_ref[...]).astype(BF16)
    acc = x
    for f in range(D_FF // FFN_TF):
        cols = slice(f * FFN_TF, (f + 1) * FFN_TF)
        u = jnp.maximum(_dot(h, wu_ref[:, cols]), 0.0)
        acc = acc + _dot((u * u).astype(BF16), wd_ref[cols, :])
    if final_norm:
        acc = _rms(acc, gf_ref[...])
    o_ref[...] = acc


def _ffn(x, g_ffn, layer, w_up, w_down, g_final, final_norm):
    m, d = x.shape
    tm = FFN_TM
    vec = pl.BlockSpec((1, d), lambda i: (0, 0))
    return pl.pallas_call(
        functools.partial(_ffn_kernel, final_norm),
        grid=(m // tm,),
        in_specs=[pl.BlockSpec((tm, d), lambda i: (i, 0)), vec,
                  _resident((None, d, D_FF), lambda i: (layer, 0, 0)),
                  _resident((None, D_FF, d), lambda i: (layer, 0, 0)), vec],
        out_specs=pl.BlockSpec((tm, d), lambda i: (i, 0)),
        out_shape=jax.ShapeDtypeStruct((m, d), F32),
        name="ffn",
        compiler_params=_params(("parallel",), 56),
    )(x, g_ffn.reshape(1, d), w_up, w_down, g_final.reshape(1, d))


def _inproj_colscale():
    cs = jnp.ones((D_IN,), F32)
    cs = cs.at[OFF_RK:OFF_RK + RET_QK_W].set(RET_DK ** -0.5)
    cs = cs.at[OFF_DQ:OFF_DQ + DIFF_QK_W].set(DIFF_DK ** -0.5)
    cs = cs.at[OFF_CQ:OFF_CQ + CROSS_W].set(CROSS_DH ** -0.5)
    return cs


def kernel(x, mem, g_mix, w_in, g_ret, w_ret_o, lambda_q1, lambda_k1, lambda_q2, lambda_k2,
           g_diff, w_diff_o, g_mem, w_mem_kv, w_cross_o, w_out, g_ffn, w_up, w_down, g_final):
    batch, seq, d = x.shape
    xf = x.reshape(batch * seq, d)
    memf = mem.reshape(batch * mem.shape[1], d)
    colscale = _inproj_colscale()
    ones_kv = jnp.ones((2 * CROSS_W,), F32)
    w_in, w_mem_kv, w_ret_o, w_diff_o, w_cross_o, w_out, w_up, w_down = (
        w.astype(BF16) for w in (w_in, w_mem_kv, w_ret_o, w_diff_o, w_cross_o, w_out, w_up, w_down))
    for l in range(DEPTH):
        lam_init = 0.8 - 0.6 * math.exp(-0.3 * l)
        z = _inproj(xf, g_mix[l], w_in, l, colscale)
        ret = _retention(z, batch, seq)
        da = _diff_attention(z, lambda_q1[l], lambda_k1[l], lambda_q2[l], lambda_k2[l],
                             g_diff[l], lam_init, batch, seq)
        kv = _inproj(memf, g_mem[l], w_mem_kv, l, ones_kv)
        ca = _cross_attention(z, kv, batch, seq)
        xf = _merge(xf, ret, da, ca, z, g_ret[l], l, w_ret_o, w_diff_o, w_cross_o, w_out)
        xf = _ffn(xf, g_ffn[l], l, w_up, w_down, g_final, l == DEPTH - 1)
    return xf.reshape(batch, seq, d)
```

```python
import functools
import math

import jax
import jax.numpy as jnp
from jax import lax
from jax.experimental import pallas as pl
from jax.experimental.pallas import tpu as pltpu

F32 = jnp.float32
BF16 = jnp.bfloat16

D_MODEL = 1024
DEPTH = 2
N_MEM = 256
RET_HEADS = 4
RET_DK = D_MODEL // RET_HEADS
RET_DV = 2 * RET_DK
RET_KCHUNK = 256
DIFF_HEADS = 4
DIFF_DK = D_MODEL // (2 * DIFF_HEADS)
DIFF_DV = 2 * DIFF_DK
CROSS_HEADS = 4
CROSS_DH = D_MODEL // CROSS_HEADS
D_FF = 4 * D_MODEL
EPS = 1e-6

RET_QK_W = RET_HEADS * RET_DK
RET_V_W = RET_HEADS * RET_DV
DIFF_QK_W = DIFF_HEADS * 2 * DIFF_DK
DIFF_V_W = DIFF_HEADS * DIFF_DV
CROSS_W = CROSS_HEADS * CROSS_DH
GATE_W = 3 * D_MODEL
OFF_RQ = 0
OFF_RK = OFF_RQ + RET_QK_W
OFF_RV = OFF_RK + RET_QK_W
OFF_RG = OFF_RV + RET_V_W
OFF_DQ = OFF_RG + RET_V_W
OFF_DK = OFF_DQ + DIFF_QK_W
OFF_DV = OFF_DK + DIFF_QK_W
OFF_CQ = OFF_DV + DIFF_V_W
OFF_GATE = OFF_CQ + CROSS_W
D_IN = OFF_GATE + GATE_W

MIB = 1024 * 1024
LANES = 128
POS_SPLIT = 256

INPROJ_TM = 1024
INPROJ_TN = 3328
INPROJ_TC = 256
RET_UNROLL = 16
DIFF_TQ = 512
DIFF_TK = 512
CROSS_TQ = 4096
MERGE_TM = 512
FFN_TM = 1024
FFN_TF = 1024


def _params(semantics, vmem_mib):
    return pltpu.CompilerParams(dimension_semantics=semantics,
                                vmem_limit_bytes=vmem_mib * MIB)


def _resident(shape, index_map):
    return pl.BlockSpec(shape, index_map, pipeline_mode=pl.Buffered(1))


def _rms(x, g):
    ms = jnp.mean(x * x, axis=-1, keepdims=True)
    return x * lax.rsqrt(ms + EPS) * g


def _sigmoid(x):
    return 0.5 * jnp.tanh(0.5 * x) + 0.5


def _dot(a, b):
    return jnp.dot(a, b, preferred_element_type=F32)


def _dot_nt(a, b):
    return lax.dot_general(a, b, (((1,), (1,)), ((), ())), preferred_element_type=F32)


def _dot_tn(a, b):
    return lax.dot_general(a, b, (((0,), (0,)), ((), ())), preferred_element_type=F32)


def _inproj_kernel(x_ref, g_ref, w_ref, cs_ref, o_ref, h_ref):
    @pl.when(pl.program_id(1) == 0)
    def _():
        h_ref[...] = _rms(x_ref[...], g_ref[...]).astype(BF16)

    for c0 in range(0, o_ref.shape[1], INPROJ_TC):
        cols = slice(c0, c0 + INPROJ_TC)
        acc = _dot(h_ref[...], w_ref[:, cols])
        o_ref[:, cols] = (acc * cs_ref[:, cols]).astype(o_ref.dtype)


def _inproj(x, g, w, layer, colscale):
    m, k = x.shape
    n = w.shape[2]
    tm = min(INPROJ_TM, m)
    tn = min(INPROJ_TN, n)
    return pl.pallas_call(
        _inproj_kernel,
        grid=(m // tm, n // tn),
        in_specs=[
            pl.BlockSpec((tm, k), lambda i, j: (i, 0)),
            pl.BlockSpec((1, k), lambda i, j: (0, 0)),
            pl.BlockSpec((None, k, tn), lambda i, j: (layer, 0, j)),
            pl.BlockSpec((1, tn), lambda i, j: (0, j)),
        ],
        out_specs=pl.BlockSpec((tm, tn), lambda i, j: (i, j)),
        out_shape=jax.ShapeDtypeStruct((m, n), BF16),
        scratch_shapes=[pltpu.VMEM((tm, k), BF16)],
        name="inproj",
        compiler_params=_params(("parallel", "arbitrary"), 56),
    )(x, g.reshape(1, k), w, colscale.reshape(1, n))


def _retention_kernel(chd_ref, q_ref, k_ref, v_ref, intra_ref, ind_ref, std_ref, o_ref, state_ref):
    h = pl.program_id(1)
    state_ref[...] = jnp.zeros_like(state_ref)
    ch_decay = chd_ref[h]
    c = RET_KCHUNK

    def lanes(table, width):
        return jnp.concatenate([table] * (width // LANES), axis=1)

    def chunk(ci):
        rows = pl.ds(pl.multiple_of(ci * c, c), c)
        q = q_ref[rows, :]
        k = k_ref[rows, :]
        v = v_ref[rows, :]
        state = state_ref[...]
        scores = _dot_nt(q, k) * intra_ref[...]
        inner = _dot(scores.astype(BF16), v)
        qd = (q.astype(F32) * lanes(ind_ref[...], RET_DK)).astype(BF16)
        cross = _dot(qd, state.astype(BF16))
        kd = (k.astype(F32) * lanes(std_ref[...], RET_DK)).astype(BF16)
        state_ref[...] = ch_decay * state + _dot_tn(kd, v)
        o_ref[rows, :] = (inner + cross).astype(o_ref.dtype)

    def group(gi, carry):
        for ci in range(RET_UNROLL):
            chunk(gi * RET_UNROLL + ci)
        return carry

    lax.fori_loop(0, q_ref.shape[0] // (c * RET_UNROLL), group, 0)


def _retention(z, batch, seq):
    hh = RET_HEADS
    c = RET_KCHUNK
    assert seq % (c * RET_UNROLL) == 0
    log_g = jnp.log(1.0 - jnp.exp2(-5.0 - jnp.arange(hh, dtype=F32)))
    idx = jnp.arange(c, dtype=F32)
    rel = idx[:, None] - idx[None, :]
    intra = jnp.where(rel >= 0, jnp.exp(log_g[:, None, None] * jnp.maximum(rel, 0.0)), 0.0)
    in_decay = jnp.broadcast_to(jnp.exp(log_g[:, None] * (idx + 1.0))[:, :, None], (hh, c, LANES))
    st_decay = jnp.broadcast_to(jnp.exp(log_g[:, None] * (c - 1.0 - idx))[:, :, None],
                                (hh, c, LANES))
    ch_decay = jnp.exp(log_g * c)

    return pl.pallas_call(
        _retention_kernel,
        grid=(batch, hh),
        in_specs=[
            pl.BlockSpec(memory_space=pltpu.SMEM),
            pl.BlockSpec((seq, RET_DK), lambda b, h: (b, OFF_RQ // RET_DK + h)),
            pl.BlockSpec((seq, RET_DK), lambda b, h: (b, OFF_RK // RET_DK + h)),
            pl.BlockSpec((seq, RET_DV), lambda b, h: (b, OFF_RV // RET_DV + h)),
            pl.BlockSpec((None, c, c), lambda b, h: (h, 0, 0)),
            pl.BlockSpec((None, c, LANES), lambda b, h: (h, 0, 0)),
            pl.BlockSpec((None, c, LANES), lambda b, h: (h, 0, 0)),
        ],
        out_specs=pl.BlockSpec((seq, RET_DV), lambda b, h: (b, h)),
        out_shape=jax.ShapeDtypeStruct((batch * seq, RET_V_W), BF16),
        scratch_shapes=[pltpu.VMEM((RET_DK, RET_DV), F32)],
        name="retention",
        compiler_params=_params(("parallel", "parallel"), 40),
    )(ch_decay, z, z, z, intra, in_decay, st_decay)


def _diff_kernel(lam_init, slopes_ref, q_ref, k_ref, v_ref, lq1_ref, lk1_ref, lq2_ref, lk2_ref,
                 g_ref, o_ref, kaug_ref, qaug_ref, s_ref, mblk_ref, m_ref, l_ref, acc_ref):
    h = pl.program_id(0)
    seq = q_ref.shape[0]
    tq = DIFF_TQ
    tk = DIFF_TK
    d = DIFF_DK
    nq = seq // tq
    slope = slopes_ref[h]

    @pl.when(pl.program_id(1) == 0)
    def _():
        def build_positions(j, carry):
            rows = pl.ds(pl.multiple_of(j * tk, tk), tk)
            pos = j * tk + lax.broadcasted_iota(jnp.int32, (tk, LANES), 0)
            lane = lax.broadcasted_iota(jnp.int32, (tk, LANES), 1)
            lo = pos & (POS_SPLIT - 1)
            kcols = jnp.where(lane == 0, (pos - lo).astype(F32) * slope,
                              jnp.where(lane == 1, lo.astype(F32) * slope,
                                        jnp.where(lane == 2, 1.0, 0.0))).astype(BF16)
            row = lax.broadcasted_iota(jnp.int32, (d, tq), 0)
            q0 = lax.convert_element_type(j * tq, F32)
            qrows = jnp.where(row < 2, 1.0, jnp.where(row == 2, -slope * q0, 0.0)).astype(BF16)
            for mp in range(2):
                kaug_ref[mp, rows, d:2 * d] = kcols
                qaug_ref[mp, j, d:2 * d] = qrows
            return carry

        lax.fori_loop(0, seq // tk, build_positions, 0)

    def build(j, carry):
        rows = pl.ds(pl.multiple_of(j * tk, tk), tk)
        for mp in range(2):
            kaug_ref[mp, rows, 0:d] = k_ref[rows, mp * d:(mp + 1) * d]
            qaug_ref[mp, j, 0:d] = q_ref[rows, mp * d:(mp + 1) * d].T
        return carry

    lax.fori_loop(0, seq // tk, build, 0)

    lam = (jnp.exp(jnp.sum(lq1_ref[...] * lk1_ref[...], keepdims=True))
           - jnp.exp(jnp.sum(lq2_ref[...] * lk2_ref[...], keepdims=True)) + lam_init)

    def scores(qi, j, nblk, slot):
        krows = pl.ds(pl.multiple_of(j * tk, tk), nblk * tk)
        for mp in range(2):
            s = _dot(kaug_ref[mp, krows, :], qaug_ref[mp, qi])
            s_ref[slot, mp, 0:nblk * tk] = s
            mblk_ref[slot, mp] = jnp.max(s, axis=0, keepdims=True)

    def softmax_pv(j, nblk, slot, diag):
        nk = nblk * tk
        rows = pl.ds(pl.multiple_of(j * tk, tk), nk)
        v = v_ref[rows, :]
        for mp in range(2):
            s = s_ref[slot, mp, 0:nk]
            m_old = m_ref[mp]
            if diag:
                key = lax.broadcasted_iota(jnp.int32, (nk, tq), 0) - (nk - tk)
                qry = lax.broadcasted_iota(jnp.int32, (nk, tq), 1)
                s = jnp.where(key <= qry, s, -jnp.inf)
                m_blk = jnp.max(s, axis=0, keepdims=True)
            else:
                m_blk = mblk_ref[slot, mp]
            m_new = jnp.maximum(m_old, m_blk)
            alpha = jnp.exp(m_old - m_new)
            p = jnp.exp(s - m_new)
            l_ref[mp] = alpha * l_ref[mp] + jnp.sum(p, axis=0, keepdims=True)
            m_ref[mp] = m_new
            acc_ref[mp] = alpha * acc_ref[mp] + _dot_tn(v, p.astype(BF16))

    def step(qi, j, nblk, slot, diag, following):
        if following is not None:
            scores(qi, following[0], following[1], 1 - slot)
        softmax_pv(j, nblk, slot, diag)

    def tile(qi, carry):
        m_ref[...] = jnp.full_like(m_ref, -jnp.inf)
        l_ref[...] = jnp.zeros_like(l_ref)
        acc_ref[...] = jnp.zeros_like(acc_ref)

        def pair(t, c):
            step(qi, 4 * t, 2, 0, False, (4 * t + 2, 2))
            step(qi, 4 * t + 2, 2, 1, False, (4 * t + 4, 2))
            return c

        lax.fori_loop(0, qi // 4, pair, 0)
        rest = qi % 4

        @pl.when(rest == 0)
        def _():
            step(qi, qi, 1, 0, True, None)

        @pl.when(rest == 1)
        def _():
            step(qi, qi - 1, 2, 0, True, None)

        @pl.when(rest == 2)
        def _():
            step(qi, qi - 2, 2, 0, False, (qi, 1))
            step(qi, qi, 1, 1, True, None)

        @pl.when(rest == 3)
        def _():
            step(qi, qi - 3, 2, 0, False, (qi - 1, 2))
            step(qi, qi - 1, 2, 1, True, None)

        scores(jnp.minimum(qi + 1, nq - 1), 0, 2, 0)
        o = acc_ref[0] / l_ref[0] - lam * (acc_ref[1] / l_ref[1])
        o = o * lax.rsqrt(jnp.mean(o * o, axis=0, keepdims=True) + EPS)
        qrows = pl.ds(pl.multiple_of(qi * tq, tq), tq)
        o_ref[qrows, :] = (o.T * g_ref[...] * (1.0 - lam_init)).astype(o_ref.dtype)
        return carry

    scores(0, 0, 1, 0)
    lax.fori_loop(0, nq, tile, 0)


def _diff_attention(z, lq1, lk1, lq2, lk2, g_diff, lam_init, batch, seq):
    hh = DIFF_HEADS
    tq = DIFF_TQ
    assert DIFF_TQ == DIFF_TK and tq & (tq - 1) == 0 and seq % tq == 0
    assert 2 * tq <= seq <= POS_SPLIT * POS_SPLIT
    w = 2 * DIFF_DK
    slopes = [2.0 ** (-8.0 * (i + 1) / hh) for i in range(hh)]
    assert all(math.log2(s).is_integer() for s in slopes)
    vec = pl.BlockSpec((1, DIFF_DK), lambda h, b: (0, 0))
    return pl.pallas_call(
        functools.partial(_diff_kernel, lam_init),
        grid=(hh, batch),
        in_specs=[
            pl.BlockSpec(memory_space=pltpu.SMEM),
            pl.BlockSpec((seq, w), lambda h, b: (b, OFF_DQ // w + h)),
            pl.BlockSpec((seq, w), lambda h, b: (b, OFF_DK // w + h)),
            pl.BlockSpec((seq, DIFF_DV), lambda h, b: (b, OFF_DV // DIFF_DV + h)),
            vec, vec, vec, vec,
            pl.BlockSpec((1, DIFF_DV), lambda h, b: (0, h)),
        ],
        out_specs=pl.BlockSpec((seq, DIFF_DV), lambda h, b: (b, h)),
        out_shape=jax.ShapeDtypeStruct((batch * seq, DIFF_V_W), BF16),
        scratch_shapes=[pltpu.VMEM((2, seq, w), BF16), pltpu.VMEM((2, seq // tq, w, tq), BF16),
                        pltpu.VMEM((2, 2, 2 * DIFF_TK, tq), F32),
                        pltpu.VMEM((2, 2, 1, tq), F32),
                        pltpu.VMEM((2, 1, tq), F32), pltpu.VMEM((2, 1, tq), F32),
                        pltpu.VMEM((2, DIFF_DV, tq), F32)],
        name="diff_attention",
        compiler_params=_params(("arbitrary", "arbitrary"), 48),
    )(jnp.asarray(slopes, F32), z, z, z, lq1.reshape(1, -1), lk1.reshape(1, -1),
      lq2.reshape(1, -1), lk2.reshape(1, -1), g_diff.reshape(1, DIFF_V_W))


def _cross_kernel(q_ref, mk_ref, mv_ref, o_ref):
    dh = CROSS_DH
    for h in range(CROSS_HEADS):
        cols = slice(h * dh, (h + 1) * dh)
        s = _dot_nt(mk_ref[:, cols], q_ref[:, cols])
        p = jnp.exp(s - jnp.max(s, axis=0, keepdims=True))
        p = p * (1.0 / jnp.sum(p, axis=0, keepdims=True))
        o_ref[:, cols] = _dot_tn(p.astype(BF16), mv_ref[:, cols]).astype(o_ref.dtype)


def _cross_attention(z, kv, batch, seq):
    tq = CROSS_TQ
    nq = seq // tq
    return pl.pallas_call(
        _cross_kernel,
        grid=(batch, nq),
        in_specs=[
            pl.BlockSpec((tq, CROSS_W), lambda b, i: (b * nq + i, OFF_CQ // CROSS_W)),
            pl.BlockSpec((N_MEM, CROSS_W), lambda b, i: (b, 0)),
            pl.BlockSpec((N_MEM, CROSS_W), lambda b, i: (b, 1)),
        ],
        out_specs=pl.BlockSpec((tq, CROSS_W), lambda b, i: (b * nq + i, 0)),
        out_shape=jax.ShapeDtypeStruct((batch * seq, CROSS_W), BF16),
        name="cross_attention",
        compiler_params=_params(("parallel", "arbitrary"), 40),
    )(z, kv, kv)


def _merge_kernel(x_ref, ret_ref, rg_ref, da_ref, ca_ref, g0_ref, g1_ref, g2_ref,
                  gr_ref, wr_ref, wd_ref, wc_ref, wo_ref, o_ref):
    merged = _sigmoid(g1_ref[...].astype(F32)) * _dot(da_ref[...], wd_ref[...])
    merged += _sigmoid(g2_ref[...].astype(F32)) * _dot(ca_ref[...], wc_ref[...])
    y_ret = None
    for h in range(RET_HEADS):
        cols = slice(h * RET_DV, (h + 1) * RET_DV)
        o = ret_ref[:, cols].astype(F32)
        oc = o - jnp.mean(o, axis=-1, keepdims=True)
        y = _rms(oc, gr_ref[:, cols])
        rg = rg_ref[:, cols].astype(F32)
        part = _dot((rg * _sigmoid(rg) * y).astype(BF16), wr_ref[cols, :])
        y_ret = part if y_ret is None else y_ret + part
    merged += _sigmoid(g0_ref[...].astype(F32)) * y_ret
    o_ref[...] = x_ref[...] + _dot(merged.astype(BF16), wo_ref[...])


def _merge(x, ret, da, ca, z, g_ret, layer, w_ret_o, w_diff_o, w_cross_o, w_out):
    m, d = x.shape
    tm = MERGE_TM
    gate0 = OFF_GATE // d

    def rows(width):
        return pl.BlockSpec((tm, width), lambda i: (i, 0))

    def gate(n):
        return pl.BlockSpec((tm, d), lambda i: (i, gate0 + n))

    def weight(k):
        return _resident((None, k, d), lambda i: (layer, 0, 0))

    return pl.pallas_call(
        _merge_kernel,
        grid=(m // tm,),
        in_specs=[rows(d), rows(RET_V_W),
                  pl.BlockSpec((tm, RET_V_W), lambda i: (i, OFF_RG // RET_V_W)),
                  rows(DIFF_V_W), rows(CROSS_W), gate(0), gate(1), gate(2),
                  pl.BlockSpec((1, RET_V_W), lambda i: (0, 0)),
                  weight(RET_V_W), weight(DIFF_V_W), weight(CROSS_W), weight(d)],
        out_specs=rows(d),
        out_shape=jax.ShapeDtypeStruct((m, d), F32),
        name="merge",
        compiler_params=_params(("parallel",), 56),
    )(x, ret, z, da, ca, z, z, z, g_ret.reshape(1, RET_V_W), w_ret_o, w_diff_o, w_cross_o, w_out)


def _ffn_kernel(final_norm, x_ref, g_ref, wu_ref, wd_ref, gf_ref, o_ref):
    x = x_ref[...]
    h = _rms(x, g_ref[...]).astype(BF16)
    acc = x
    for f in range(D_FF // FFN_TF):
        cols = slice(f * FFN_TF, (f + 1) * FFN_TF)
        u = jnp.maximum(_dot(h, wu_ref[:, cols]), 0.0)
        acc = acc + _dot((u * u).astype(BF16), wd_ref[cols, :])
    if final_norm:
        acc = _rms(acc, gf_ref[...])
    o_ref[...] = acc


def _ffn(x, g_ffn, layer, w_up, w_down, g_final, final_norm):
    m, d = x.shape
    tm = FFN_TM
    vec = pl.BlockSpec((1, d), lambda i: (0, 0))
    return pl.pallas_call(
        functools.partial(_ffn_kernel, final_norm),
        grid=(m // tm,),
        in_specs=[pl.BlockSpec((tm, d), lambda i: (i, 0)), vec,
                  _resident((None, d, D_FF), lambda i: (layer, 0, 0)),
                  _resident((None, D_FF, d), lambda i: (layer, 0, 0)), vec],
        out_specs=pl.BlockSpec((tm, d), lambda i: (i, 0)),
        out_shape=jax.ShapeDtypeStruct((m, d), F32),
        name="ffn",
        compiler_params=_params(("parallel",), 56),
    )(x, g_ffn.reshape(1, d), w_up, w_down, g_final.reshape(1, d))


def _inproj_colscale():
    cs = jnp.ones((D_IN,), F32)
    cs = cs.at[OFF_RK:OFF_RK + RET_QK_W].set(RET_DK ** -0.5)
    cs = cs.at[OFF_DQ:OFF_DQ + DIFF_QK_W].set(DIFF_DK ** -0.5)
    cs = cs.at[OFF_CQ:OFF_CQ + CROSS_W].set(CROSS_DH ** -0.5)
    return cs


def kernel(x, mem, g_mix, w_in, g_ret, w_ret_o, lambda_q1, lambda_k1, lambda_q2, lambda_k2,
           g_diff, w_diff_o, g_mem, w_mem_kv, w_cross_o, w_out, g_ffn, w_up, w_down, g_final):
    batch, seq, d = x.shape
    xf = x.reshape(batch * seq, d)
    memf = mem.reshape(batch * mem.shape[1], d)
    colscale = _inproj_colscale()
    ones_kv = jnp.ones((2 * CROSS_W,), F32)
    w_in, w_mem_kv, w_ret_o, w_diff_o, w_cross_o, w_out, w_up, w_down = (
        w.astype(BF16) for w in (w_in, w_mem_kv, w_ret_o, w_diff_o, w_cross_o, w_out, w_up, w_down))
    for l in range(DEPTH):
        lam_init = 0.8 - 0.6 * math.exp(-0.3 * l)
        z = _inproj(xf, g_mix[l], w_in, l, colscale)
        ret = _retention(z, batch, seq)
        da = _diff_attention(z, lambda_q1[l], lambda_k1[l], lambda_q2[l], lambda_k2[l],
                             g_diff[l], lam_init, batch, seq)
        kv = _inproj(memf, g_mem[l], w_mem_kv, l, ones_kv)
        ca = _cross_attention(z, kv, batch, seq)
        xf = _merge(xf, ret, da, ca, z, g_ret[l], l, w_ret_o, w_diff_o, w_cross_o, w_out)
        xf = _ffn(xf, g_ffn[l], l, w_up, w_down, g_final, l == DEPTH - 1)
    return xf.reshape(batch, seq, d)
```
